```python
import jax
import jax.numpy as jnp
from jax import lax
import numpy as np

D_MODEL = 1024
BATCH = 8
SEQ = 4096
DEPTH = 4
DEC_BATCH = 32
DEC_SEQ = 64
PAST_LEN = 2048

CHUNK = 64
N_MIXERS = 4
N_CONF = (DEPTH + 3) // 4
N_POOL = (DEPTH + 2) // 4
N_SCONV = (DEPTH + 1) // 4
N_HGRN = DEPTH // 4
CONF_WIDTH = 31
POOL_WINDOWS = (2, 4, 8, 16)
POOL_GROUPS = 4
POOL_GROUP_DIM = D_MODEL // POOL_GROUPS
POOL_HIST = max(POOL_WINDOWS) - 1
SCONV_WIDTH = 3
HGRN_HEAD_DIM = 128
HGRN_HEADS = D_MODEL // HGRN_HEAD_DIM
D_FF = 2816
FFN_WIDTH = 3
EPS = 1e-6

kernel_name = "hybrid_streaming_encoder_step"


def rmsnorm(x, g):
    xf = x.astype(jnp.float32)
    y = xf * lax.rsqrt(jnp.mean(xf * xf, axis=-1, keepdims=True) + EPS)
    return (y * g.astype(jnp.float32)).astype(x.dtype)


def causal_dwconv(x, hist, w):
    width = w.shape[0]
    xp = jnp.concatenate([hist.astype(x.dtype), x], axis=1)
    y = lax.conv_general_dilated(xp, w[:, None, :].astype(x.dtype), window_strides=(1,),
                                 padding='VALID', dimension_numbers=('NWC', 'WIO', 'NWC'),
                                 feature_group_count=x.shape[-1])
    return y, xp[:, xp.shape[1] - (width - 1):]


def conformer_conv(xn, hist, w1, b1, wdw, bdw, ln_g, ln_b, w2, b2):
    a, gate = jnp.split(xn @ w1 + b1, 2, axis=-1)
    u = a * jax.nn.sigmoid(gate)
    c, new_hist = causal_dwconv(u, hist, wdw)
    cf = (c + bdw).astype(jnp.float32)
    mu = jnp.mean(cf, axis=-1, keepdims=True)
    var = jnp.mean(jnp.square(cf - mu), axis=-1, keepdims=True)
    cn = ((cf - mu) * lax.rsqrt(var + EPS) * ln_g + ln_b).astype(xn.dtype)
    return jax.nn.silu(cn) @ w2 + b2, new_hist


def pool_mixer(xn, hist, pos0, w_pool, scale):
    b, t, d = xn.shape
    xp = jnp.concatenate([hist.astype(xn.dtype), xn], axis=1).astype(jnp.float32)
    c = jnp.concatenate([jnp.zeros((b, 1, d), jnp.float32), jnp.cumsum(xp, axis=1)], axis=1)
    end = c[:, POOL_HIST + 1:]
    pos = pos0 + jnp.arange(t)
    outs = []
    for g, w in enumerate(POOL_WINDOWS):
        lo, hi = g * POOL_GROUP_DIM, (g + 1) * POOL_GROUP_DIM
        start = c[:, POOL_HIST + 1 - w:POOL_HIST + 1 - w + t, lo:hi]
        cnt = jnp.minimum(pos + 1, w).astype(jnp.float32)[None, :, None]
        diff = (end[..., lo:hi] - start) / cnt - xp[:, POOL_HIST:, lo:hi]
        outs.append(diff.astype(xn.dtype) @ w_pool[g])
    y = jnp.concatenate(outs, axis=-1) * scale
    return y, xp[:, xp.shape[1] - POOL_HIST:].astype(xn.dtype)


def short_conv_mixer(xn, hist, w_in, w_dw, w_out):
    bg, cg, v = jnp.split(xn @ w_in, 3, axis=-1)
    c, new_hist = causal_dwconv(cg * v, hist, w_dw)
    return (bg * c) @ w_out, new_hist


def hgrn2_chunk(s, inp):
    q, k, v, logf = inp
    L = q.shape[1]
    bcum = jnp.cumsum(logf, axis=1)
    qd = q * jnp.exp(bcum)
    kd = k * jnp.exp(-bcum)
    scores = jnp.einsum('bthk,bshk->bhts', qd, kd)
    mask = jnp.tril(jnp.ones((L, L), dtype=bool))
    scores = jnp.where(mask, scores, 0.0)
    o = jnp.einsum('bhts,bshv->bthv', scores, v) + jnp.einsum('bthk,bhkv->bthv', qd, s)
    b_last = bcum[:, -1]
    s_new = s * jnp.exp(b_last)[..., None] + jnp.einsum(
        'bshk,bshv->bhkv', k * jnp.exp(b_last[:, None] - bcum), v)
    return s_new, o


def hgrn2_mixer(xn, s0, lb, w_q, w_f, w_i, w_g, w_o, norm_g):
    b, t, d = xn.shape
    shp = (b, t, HGRN_HEADS, HGRN_HEAD_DIM)
    q = jax.nn.silu((xn @ w_q).astype(jnp.float32)).reshape(shp)
    lbh = lb.astype(jnp.float32).reshape(HGRN_HEADS, HGRN_HEAD_DIM)
    f = lbh + (1.0 - lbh) * jax.nn.sigmoid((xn @ w_f).astype(jnp.float32)).reshape(shp)
    k = 1.0 - f
    logf = jnp.log(f)
    v = (xn @ w_i).astype(jnp.float32).reshape(shp)
    s0 = s0.astype(jnp.float32)
    if t <= CHUNK:
        s_new, o = hgrn2_chunk(s0, (q, k, v, logf))
    else:
        n = t // CHUNK
        to_blocks = lambda a: jnp.moveaxis(a.reshape(b, n, CHUNK, HGRN_HEADS, HGRN_HEAD_DIM), 1, 0)
        s_new, o = lax.scan(hgrn2_chunk, s0, (to_blocks(q), to_blocks(k), to_blocks(v), to_blocks(logf)))
        o = jnp.moveaxis(o, 0, 1).reshape(shp)
    on = o * lax.rsqrt(jnp.mean(o * o, axis=-1, keepdims=True) + EPS) * norm_g.astype(jnp.float32)
    on = on.reshape(b, t, d).astype(xn.dtype) * jax.nn.silu(xn @ w_g)
    return on @ w_o, s_new


def conv_ffn(xn, hist, w_gate, w_up, w_dw, b_dw, w_down):
    g, new_hist = causal_dwconv(xn @ w_gate, hist, w_dw)
    h = jax.nn.silu(g + b_dw) * (xn @ w_up)
    return h @ w_down, new_hist


def trunk(x, pos0, conf_hist, pool_hist, sconv_hist, hgrn_state, ffn_hist, p):
    lb_all = jnp.cumsum(jax.nn.softmax(p['hgrn_lb_logits'].astype(jnp.float32), axis=0), axis=0)
    lb_all = lb_all - lb_all[0:1]
    new_conf, new_pool, new_sconv, new_hgrn, new_ffn = [], [], [], [], []
    for i in range(DEPTH):
        m, j = i % N_MIXERS, i // N_MIXERS
        xn = rmsnorm(x, p['norm_mix'][i])
        if m == 0:
            y, h = conformer_conv(xn, conf_hist[j], p['conf_w_pw1'][j], p['conf_b_pw1'][j],
                                  p['conf_w_dw'][j], p['conf_b_dw'][j], p['conf_ln_g'][j],
                                  p['conf_ln_b'][j], p['conf_w_pw2'][j], p['conf_b_pw2'][j])
            new_conf.append(h)
        elif m == 1:
            y, h = pool_mixer(xn, pool_hist[j], pos0, p['pool_w'][j], p['pool_scale'][j])
            new_pool.append(h)
        elif m == 2:
            y, h = short_conv_mixer(xn, sconv_hist[j], p['sconv_w_in'][j], p['sconv_w_dw'][j],
                                    p['sconv_w_out'][j])
            new_sconv.append(h)
        else:
            y, h = hgrn2_mixer(xn, hgrn_state[j], lb_all[i], p['hgrn_w_q'][j], p['hgrn_w_f'][j],
                               p['hgrn_w_i'][j], p['hgrn_w_g'][j], p['hgrn_w_o'][j],
                               p['hgrn_norm_g'][j])
            new_hgrn.append(h.astype(hgrn_state.dtype))
        x = x + y
        xn = rmsnorm(x, p['norm_ffn'][i])
        y, h = conv_ffn(xn, ffn_hist[i], p['ffn_w_gate'][i], p['ffn_w_up'][i], p['ffn_w_dw'][i],
                        p['ffn_b_dw'][i], p['ffn_w_down'][i])
        new_ffn.append(h)
        x = x + y
    return (rmsnorm(x, p['norm_final']), jnp.stack(new_conf), jnp.stack(new_pool),
            jnp.stack(new_sconv), jnp.stack(new_hgrn), jnp.stack(new_ffn))


def setup_inputs(seed: int = 0) -> dict:
    key = jax.random.key(seed)
    keys = list(jax.random.split(key, 40))

    def nrm(idx, shape, s):
        return jax.random.normal(keys[idx], shape, jnp.float32) * s

    D = D_MODEL
    return {
        'x_prompt': nrm(0, (BATCH, SEQ, D), 1.0),
        'x_sample': nrm(1, (DEC_BATCH, DEC_SEQ, D), 1.0),
        'state_conformer_conv': nrm(2, (N_CONF, DEC_BATCH, CONF_WIDTH - 1, D), 0.5),
        'state_pool': nrm(3, (N_POOL, DEC_BATCH, POOL_HIST, D), 1.0),
        'state_short_conv': nrm(4, (N_SCONV, DEC_BATCH, SCONV_WIDTH - 1, D), 0.5),
        'state_hgrn': nrm(5, (N_HGRN, DEC_BATCH, HGRN_HEADS, HGRN_HEAD_DIM, HGRN_HEAD_DIM), 0.5),
        'state_ffn_conv': nrm(6, (DEPTH, DEC_BATCH, FFN_WIDTH - 1, D_FF), 1.0),
        'norm_mix': 1.0 + nrm(7, (DEPTH, D), 0.05),
        'norm_ffn': 1.0 + nrm(8, (DEPTH, D), 0.05),
        'norm_final': 1.0 + nrm(9, (D,), 0.05),
        'conf_w_pw1': nrm(10, (N_CONF, D, 2 * D), D ** -0.5),
        'conf_b_pw1': nrm(11, (N_CONF, 2 * D), 0.02),
        'conf_w_dw': nrm(12, (N_CONF, CONF_WIDTH, D), CONF_WIDTH ** -0.5),
        'conf_b_dw': nrm(13, (N_CONF, D), 0.02),
        'conf_ln_g': 1.0 + nrm(14, (N_CONF, D), 0.05),
        'conf_ln_b': nrm(15, (N_CONF, D), 0.02),
        'conf_w_pw2': nrm(16, (N_CONF, D, D), D ** -0.5),
        'conf_b_pw2': nrm(17, (N_CONF, D), 0.02),
        'pool_w': nrm(18, (N_POOL, POOL_GROUPS, POOL_GROUP_DIM, POOL_GROUP_DIM), POOL_GROUP_DIM ** -0.5),
        'pool_scale': 0.5 + nrm(19, (N_POOL, D), 0.05),
        'sconv_w_in': nrm(20, (N_SCONV, D, 3 * D), D ** -0.5),
        'sconv_w_dw': nrm(21, (N_SCONV, SCONV_WIDTH, D), SCONV_WIDTH ** -0.5),
        'sconv_w_out': nrm(22, (N_SCONV, D, D), D ** -0.5),
        'hgrn_w_q': nrm(23, (N_HGRN, D, D), D ** -0.5),
        'hgrn_w_f': nrm(24, (N_HGRN, D, D), D ** -0.5),
        'hgrn_w_i': nrm(25, (N_HGRN, D, D), D ** -0.5),
        'hgrn_w_g': nrm(26, (N_HGRN, D, D), D ** -0.5),
        'hgrn_w_o': nrm(27, (N_HGRN, D, D), D ** -0.5),
        'hgrn_lb_logits': nrm(28, (DEPTH, D), 0.1),
        'hgrn_norm_g': 1.0 + nrm(29, (N_HGRN, HGRN_HEAD_DIM), 0.05),
        'ffn_w_gate': nrm(30, (DEPTH, D, D_FF), D ** -0.5),
        'ffn_w_up': nrm(31, (DEPTH, D, D_FF), D ** -0.5),
        'ffn_w_dw': nrm(32, (DEPTH, FFN_WIDTH, D_FF), FFN_WIDTH ** -0.5),
        'ffn_b_dw': nrm(33, (DEPTH, D_FF), 0.02),
        'ffn_w_down': nrm(34, (DEPTH, D_FF, D), D_FF ** -0.5),
    }


def reference(x_prompt, x_sample, state_conformer_conv, state_pool, state_short_conv, state_hgrn,
              state_ffn_conv, norm_mix, norm_ffn, norm_final, conf_w_pw1, conf_b_pw1, conf_w_dw,
              conf_b_dw, conf_ln_g, conf_ln_b, conf_w_pw2, conf_b_pw2, pool_w, pool_scale,
              sconv_w_in, sconv_w_dw, sconv_w_out, hgrn_w_q, hgrn_w_f, hgrn_w_i, hgrn_w_g, hgrn_w_o,
              hgrn_lb_logits, hgrn_norm_g, ffn_w_gate, ffn_w_up, ffn_w_dw, ffn_b_dw, ffn_w_down):
    p = {
        'norm_mix': norm_mix, 'norm_ffn': norm_ffn, 'norm_final': norm_final,
        'conf_w_pw1': conf_w_pw1, 'conf_b_pw1': conf_b_pw1, 'conf_w_dw': conf_w_dw,
        'conf_b_dw': conf_b_dw, 'conf_ln_g': conf_ln_g, 'conf_ln_b': conf_ln_b,
        'conf_w_pw2': conf_w_pw2, 'conf_b_pw2': conf_b_pw2,
        'pool_w': pool_w, 'pool_scale': pool_scale,
        'sconv_w_in': sconv_w_in, 'sconv_w_dw': sconv_w_dw, 'sconv_w_out': sconv_w_out,
        'hgrn_w_q': hgrn_w_q, 'hgrn_w_f': hgrn_w_f, 'hgrn_w_i': hgrn_w_i, 'hgrn_w_g': hgrn_w_g,
        'hgrn_w_o': hgrn_w_o, 'hgrn_lb_logits': hgrn_lb_logits, 'hgrn_norm_g': hgrn_norm_g,
        'ffn_w_gate': ffn_w_gate, 'ffn_w_up': ffn_w_up, 'ffn_w_dw': ffn_w_dw,
        'ffn_b_dw': ffn_b_dw, 'ffn_w_down': ffn_w_down,
    }
    b, dt = x_prompt.shape[0], x_prompt.dtype
    zc = jnp.zeros((N_CONF, b, CONF_WIDTH - 1, D_MODEL), dt)
    zp = jnp.zeros((N_POOL, b, POOL_HIST, D_MODEL), dt)
    zs = jnp.zeros((N_SCONV, b, SCONV_WIDTH - 1, D_MODEL), dt)
    zh = jnp.zeros((N_HGRN, b, HGRN_HEADS, HGRN_HEAD_DIM, HGRN_HEAD_DIM), dt)
    zf = jnp.zeros((DEPTH, b, FFN_WIDTH - 1, D_FF), dt)
    y_prompt, conf_p, pool_p, sconv_p, hgrn_p, ffn_p = trunk(x_prompt, 0, zc, zp, zs, zh, zf, p)
    y_sample, conf_s, pool_s, sconv_s, hgrn_s, ffn_s = trunk(
        x_sample, PAST_LEN, state_conformer_conv, state_pool, state_short_conv, state_hgrn,
        state_ffn_conv, p)
    return (y_prompt, y_sample, conf_p, conf_s, pool_p, pool_s, sconv_p, sconv_s,
            hgrn_p, hgrn_s, ffn_p, ffn_s)
```

```python
import functools

import jax
import jax.numpy as jnp
from jax import lax
from jax.experimental import pallas as pl
from jax.experimental.pallas import tpu as pltpu

D_MODEL = 1024
D_FF = 2816
CONF_WIDTH = 31
POOL_WINDOWS = (2, 4, 8, 16)
POOL_GROUP_DIM = D_MODEL // len(POOL_WINDOWS)
POOL_HIST = max(POOL_WINDOWS) - 1
SCONV_WIDTH = 3
FFN_WIDTH = 3
HGRN_HEAD_DIM = 128
HGRN_HEADS = D_MODEL // HGRN_HEAD_DIM
HGRN_CHUNK = 64
PAST_LEN = 2048
EPS = 1e-6

SUBLANES = 8
LANES = 128
TILE_ROWS = 512
VMEM_LIMIT_BYTES = 56 * 1024 * 1024

F32 = jnp.float32
BF16 = jnp.bfloat16


def _round_up(n, m):
    return (n + m - 1) // m * m


def _rms(x, g):
    return x * lax.rsqrt(jnp.mean(x * x, axis=-1, keepdims=True) + EPS) * g


def _dot(a, b):
    return jnp.dot(a, b, preferred_element_type=F32)


def _dot_nt(a, b):
    return lax.dot_general(a, b, (((1,), (1,)), ((), ())), preferred_element_type=F32)


def _dot_tn(a, b):
    return lax.dot_general(a, b, (((0,), (0,)), ((), ())), preferred_element_type=F32)


def _col_chunks(total, width):
    return [(lo, min(lo + width, total)) for lo in range(0, total, width)]


FFN_PAD = SUBLANES


def _ffn_kernel(x_ref, hist_ref, ng_ref, wg_ref, wu_ref, wdw_ref, bdw_ref, wd_ref, fin_ref,
                o_ref, nh_ref, cbuf, *, nb, tt, chunks, final_norm):
    m = nb * tt
    h0 = FFN_PAD - (FFN_WIDTH - 1)

    @pl.when(pl.program_id(1) == 0)
    def _():
        cbuf[:, h0:FFN_PAD, :] = hist_ref[...]

    x = x_ref[...].reshape(m, D_MODEL)
    xn = _rms(x, ng_ref[...]).astype(BF16)
    acc = x
    for lo, hi in chunks:
        nc = hi - lo
        g = _dot(xn, wg_ref[:, lo:hi]).reshape(nb, tt, nc)
        cbuf[:, FFN_PAD:FFN_PAD + tt, lo:hi] = g
        g1 = cbuf[:, FFN_PAD - 1:FFN_PAD - 1 + tt, lo:hi]
        g2 = cbuf[:, FFN_PAD - 2:FFN_PAD - 2 + tt, lo:hi]
        w = wdw_ref[:, lo:hi]
        c = g * w[2:3] + g1 * w[1:2] + g2 * w[0:1] + bdw_ref[:, lo:hi]
        cbuf[:, h0:FFN_PAD, lo:hi] = cbuf[:, h0 + tt:FFN_PAD + tt, lo:hi]
        u = _dot(xn, wu_ref[:, lo:hi])
        h = (c * jax.nn.sigmoid(c)).reshape(m, nc) * u
        acc = acc + _dot(h.astype(BF16), wd_ref[lo:hi, :])
    nh_ref[...] = cbuf[:, h0:FFN_PAD, :]
    if final_norm:
        acc = _rms(acc, fin_ref[...])
    o_ref[...] = acc.reshape(nb, tt, D_MODEL)


CONF_PAD = _round_up(CONF_WIDTH - 1, SUBLANES)


def _conf_kernel(x_ref, hist_ref, ng_ref, w1_ref, b1_ref, wdw_ref, bdw_ref, lng_ref, lnb_ref,
                 w2_ref, b2_ref, o_ref, nh_ref, ubuf, cbuf, *, nb, tt, chunks):
    m = nb * tt
    hw = CONF_WIDTH - 1
    h0 = CONF_PAD - hw

    @pl.when(pl.program_id(1) == 0)
    def _():
        ubuf[:, h0:CONF_PAD, :] = hist_ref[...]

    x = x_ref[...].reshape(m, D_MODEL)
    xn = _rms(x, ng_ref[...]).astype(BF16)
    for lo, hi in chunks:
        a = _dot(xn, w1_ref[:, lo:hi]) + b1_ref[:, lo:hi]
        gt = _dot(xn, w1_ref[:, D_MODEL + lo:D_MODEL + hi]) + b1_ref[:, D_MODEL + lo:D_MODEL + hi]
        u = a * jax.nn.sigmoid(gt)
        ubuf[:, CONF_PAD:CONF_PAD + tt, lo:hi] = u.reshape(nb, tt, hi - lo)

    def lane_block(j, carry):
        l0 = pl.multiple_of(j * LANES, LANES)
        w = wdw_ref[:, pl.ds(l0, LANES)]
        wk = [jnp.broadcast_to(w[k:k + 1], (SUBLANES, LANES)) for k in range(CONF_WIDTH)]
        bias = jnp.broadcast_to(bdw_ref[:, pl.ds(l0, LANES)], (SUBLANES, LANES))
        for b in range(nb):
            for r0 in range(0, tt, SUBLANES):
                acc = bias
                for k in range(CONF_WIDTH):
                    acc = acc + ubuf[b, h0 + k + r0:h0 + k + r0 + SUBLANES, pl.ds(l0, LANES)] * wk[k]
                cbuf[b, r0:r0 + SUBLANES, pl.ds(l0, LANES)] = acc
        return carry

    lax.fori_loop(0, D_MODEL // LANES, lane_block, 0)

    nh_ref[...] = ubuf[:, h0 + tt:CONF_PAD + tt, :]
    ubuf[:, h0:CONF_PAD, :] = ubuf[:, h0 + tt:CONF_PAD + tt, :]

    c = cbuf[...].reshape(m, D_MODEL)
    mu = jnp.mean(c, axis=-1, keepdims=True)
    cc = c - mu
    var = jnp.mean(cc * cc, axis=-1, keepdims=True)
    cn = cc * lax.rsqrt(var + EPS) * lng_ref[...] + lnb_ref[...]
    s = cn * jax.nn.sigmoid(cn)
    y = _dot(s.astype(BF16), w2_ref[...]) + b2_ref[...]
    o_ref[...] = (x + y).reshape(nb, tt, D_MODEL)


POOL_PAD = _round_up(POOL_HIST, SUBLANES)


def _pool_kernel(x_ref, hist_ref, ng_ref, wp_ref, sc_ref, o_ref, nh_ref, pbuf, dbuf,
                 *, nb, tt, pos0):
    m = nb * tt
    h0 = POOL_PAD - POOL_HIST
    t = pl.program_id(1)

    @pl.when(t == 0)
    def _():
        pbuf[:, h0:POOL_PAD, :] = hist_ref[...]

    x = x_ref[...].reshape(m, D_MODEL)
    xn = _rms(x, ng_ref[...])
    pbuf[:, POOL_PAD:POOL_PAD + tt, :] = xn.reshape(nb, tt, D_MODEL)

    pos = pos0 + t * tt + lax.broadcasted_iota(jnp.int32, (tt, 1), 0)
    for gi, win in enumerate(POOL_WINDOWS):
        inv = 1.0 / jnp.minimum(pos + 1, win).astype(F32)
        for l0 in range(gi * POOL_GROUP_DIM, (gi + 1) * POOL_GROUP_DIM, LANES):
            for b in range(nb):
                for r0 in range(0, tt, SUBLANES):
                    cur = pbuf[b, POOL_PAD + r0:POOL_PAD + r0 + SUBLANES, l0:l0 + LANES]
                    acc = cur
                    for i in range(1, win):
                        acc = acc + pbuf[b, POOL_PAD + r0 - i:POOL_PAD + r0 - i + SUBLANES, l0:l0 + LANES]
                    dbuf[b, r0:r0 + SUBLANES, l0:l0 + LANES] = acc * inv[r0:r0 + SUBLANES] - cur

    nh_ref[...] = pbuf[:, h0 + tt:POOL_PAD + tt, :]
    pbuf[:, h0:POOL_PAD, :] = pbuf[:, h0 + tt:POOL_PAD + tt, :]

    for gi in range(len(POOL_WINDOWS)):
        lo, hi = gi * POOL_GROUP_DIM, (gi + 1) * POOL_GROUP_DIM
        d = dbuf[:, :, lo:hi].reshape(m, POOL_GROUP_DIM).astype(BF16)
        y = _dot(d, wp_ref[gi]) * sc_ref[:, lo:hi]
        o_ref[:, :, lo:hi] = (x[:, lo:hi] + y).reshape(nb, tt, POOL_GROUP_DIM)


SCONV_PAD = SUBLANES


def _sconv_kernel(x_ref, hist_ref, ng_ref, win_ref, wdw_ref, wout_ref, o_ref, nh_ref, pbuf,
                  *, nb, tt, chunks):
    m = nb * tt
    h0 = SCONV_PAD - (SCONV_WIDTH - 1)

    @pl.when(pl.program_id(1) == 0)
    def _():
        pbuf[:, h0:SCONV_PAD, :] = hist_ref[...]

    x = x_ref[...].reshape(m, D_MODEL)
    xn = _rms(x, ng_ref[...]).astype(BF16)
    acc = x
    for lo, hi in chunks:
        nc = hi - lo
        bg = _dot(xn, win_ref[:, lo:hi])
        cg = _dot(xn, win_ref[:, D_MODEL + lo:D_MODEL + hi])
        v = _dot(xn, win_ref[:, 2 * D_MODEL + lo:2 * D_MODEL + hi])
        p = (cg * v).reshape(nb, tt, nc)
        pbuf[:, SCONV_PAD:SCONV_PAD + tt, lo:hi] = p
        p1 = pbuf[:, SCONV_PAD - 1:SCONV_PAD - 1 + tt, lo:hi]
        p2 = pbuf[:, SCONV_PAD - 2:SCONV_PAD - 2 + tt, lo:hi]
        w = wdw_ref[:, lo:hi]
        c = p * w[2:3] + p1 * w[1:2] + p2 * w[0:1]
        pbuf[:, h0:SCONV_PAD, lo:hi] = pbuf[:, h0 + tt:SCONV_PAD + tt, lo:hi]
        acc = acc + _dot((bg * c.reshape(m, nc)).astype(BF16), wout_ref[lo:hi, :])
    nh_ref[...] = pbuf[:, h0:SCONV_PAD, :]
    o_ref[...] = acc.reshape(nb, tt, D_MODEL)


HGRN_PROJ = 4 * HGRN_HEAD_DIM


def _hgrn_kernel(x_ref, s0_ref, ng_ref, w4_ref, wo_ref, lbl_ref, hng_ref, o_ref, sout_ref,
                 st_ref, onbuf, *, nb, tt, layer):
    m = nb * tt
    hd = HGRN_HEAD_DIM
    ck = HGRN_CHUNK
    t = pl.program_id(1)

    @pl.when(t == 0)
    def _():
        for b in range(nb):
            for h in range(HGRN_HEADS):
                st_ref[b, h] = s0_ref[b, h].T

    x = x_ref[...].reshape(m, D_MODEL)
    xn = _rms(x, ng_ref[...]).astype(BF16)

    lg = lbl_ref[...]
    e = jnp.exp(lg - jnp.max(lg, axis=0, keepdims=True))
    p = e / jnp.sum(e, axis=0, keepdims=True)
    lb_all = p[1:2]
    for i in range(2, layer + 1):
        lb_all = lb_all + p[i:i + 1]
    if layer == 0:
        lb_all = jnp.zeros_like(p[0:1])

    row = lax.broadcasted_iota(jnp.int32, (ck, ck), 0)
    col = lax.broadcasted_iota(jnp.int32, (ck, ck), 1)
    causal = row >= col
    tri = causal.astype(BF16)
    tri3 = jnp.concatenate([tri, tri, tri], axis=1)
    hng = hng_ref[...]

    for h in range(HGRN_HEADS):
        z = _dot(xn, w4_ref[:, h * HGRN_PROJ:(h + 1) * HGRN_PROJ])
        lb = lb_all[:, h * hd:(h + 1) * hd]
        zq = z[:, 0:hd]
        q = zq * jax.nn.sigmoid(zq)
        f = lb + (1.0 - lb) * jax.nn.sigmoid(z[:, hd:2 * hd])
        k = 1.0 - f
        lf = jnp.log(f)
        v = z[:, 2 * hd:3 * hd]
        zg = z[:, 3 * hd:4 * hd]
        gate = zg * jax.nn.sigmoid(zg)
        for b in range(nb):
            for c0 in range(0, tt, ck):
                r0 = b * tt + c0
                lf_c = lf[r0:r0 + ck]
                hi = lf_c.astype(BF16)
                r1 = lf_c - hi.astype(F32)
                mid = r1.astype(BF16)
                lo = (r1 - mid.astype(F32)).astype(BF16)
                bc = _dot(tri3, jnp.concatenate([hi, mid, lo], axis=0))
                bl = bc[ck - 1:ck]
                q_c = q[r0:r0 + ck]
                k_c = k[r0:r0 + ck]
                v_c = v[r0:r0 + ck].astype(BF16)
                qd = (q_c * jnp.exp(bc)).astype(BF16)
                kd = (k_c * jnp.exp(-bc)).astype(BF16)
                kk = (k_c * jnp.exp(bl - bc)).astype(BF16)
                sc = jnp.where(causal, _dot_nt(qd, kd), 0.0)
                st = st_ref[b, h]
                o = _dot(sc.astype(BF16), v_c) + _dot_nt(qd, st.astype(BF16))
                st_ref[b, h] = st * jnp.exp(bl) + _dot_tn(v_c, kk)
                on = o * lax.rsqrt(jnp.mean(o * o, axis=-1, keepdims=True) + EPS) * hng
                onbuf[r0:r0 + ck, h * hd:(h + 1) * hd] = (on * gate[r0:r0 + ck]).astype(BF16)

    y = _dot(onbuf[...], wo_ref[...])
    o_ref[...] = (x + y).reshape(nb, tt, D_MODEL)

    @pl.when(t == pl.num_programs(1) - 1)
    def _():
        for b in range(nb):
            for h in range(HGRN_HEADS):
                sout_ref[b, h] = st_ref[b, h].T


def _tiling(batch, seq):
    tt = min(seq, TILE_ROWS)
    nb = max(1, min(batch, TILE_ROWS // tt))
    assert seq % tt == 0 and batch % nb == 0 and tt % HGRN_CHUNK == 0
    return nb, tt


def _resident(shape):
    zeros = (0,) * len(shape)
    return pl.BlockSpec(shape, lambda b, t: zeros, pipeline_mode=pl.Buffered(1))


def _x_spec(nb, tt):
    return pl.BlockSpec((nb, tt, D_MODEL), lambda b, t: (b, t, 0))


def _state_spec(nb, shape):
    zeros = (0,) * len(shape)
    return pl.BlockSpec((nb,) + shape, lambda b, t: (b,) + zeros)


def _call(body, name, x, state, consts, scratch, extra_out_shape):
    batch, seq, _ = x.shape
    nb, tt = _tiling(batch, seq)
    state_shape = state.shape[1:]
    return pl.pallas_call(
        functools.partial(body, nb=nb, tt=tt),
        name=name,
        grid=(batch // nb, seq // tt),
        in_specs=[_x_spec(nb, tt), _state_spec(nb, state_shape)] + [_resident(c.shape) for c in consts],
        out_specs=[_x_spec(nb, tt), _state_spec(nb, extra_out_shape)],
        out_shape=[jax.ShapeDtypeStruct(x.shape, F32),
                   jax.ShapeDtypeStruct((batch,) + extra_out_shape, F32)],
        scratch_shapes=scratch(nb, tt),
        compiler_params=pltpu.CompilerParams(
            dimension_semantics=("arbitrary", "arbitrary"),
            vmem_limit_bytes=VMEM_LIMIT_BYTES),
    )(x, state, *consts)


def _row(v):
    return v.reshape(1, -1)


def _ffn(x, hist, ng, wg, wu, wdw, bdw, wd, fin, final_norm):
    body = functools.partial(_ffn_kernel, chunks=_col_chunks(D_FF, 512), final_norm=final_norm)
    scratch = lambda nb, tt: [pltpu.VMEM((nb, FFN_PAD + tt, D_FF), F32)]
    return _call(body, "conv_ffn", x, hist,
                 [_row(ng), wg, wu, wdw, _row(bdw), wd, _row(fin)], scratch, (FFN_WIDTH - 1, D_FF))


def _conf(x, hist, ng, w1, b1, wdw, bdw, lng, lnb, w2, b2):
    body = functools.partial(_conf_kernel, chunks=_col_chunks(D_MODEL, 512))
    scratch = lambda nb, tt: [pltpu.VMEM((nb, CONF_PAD + tt, D_MODEL), F32),
                              pltpu.VMEM((nb, tt, D_MODEL), F32)]
    return _call(body, "conformer_conv", x, hist,
                 [_row(ng), w1, _row(b1), wdw, _row(bdw), _row(lng), _row(lnb), w2, _row(b2)],
                 scratch, (CONF_WIDTH - 1, D_MODEL))


def _pool(x, hist, ng, wp, scale, pos0):
    body = functools.partial(_pool_kernel, pos0=pos0)
    scratch = lambda nb, tt: [pltpu.VMEM((nb, POOL_PAD + tt, D_MODEL), F32),
                              pltpu.VMEM((nb, tt, D_MODEL), F32)]
    return _call(body, "pool_mixer", x, hist, [_row(ng), wp, _row(scale)], scratch,
                 (POOL_HIST, D_MODEL))


def _sconv(x, hist, ng, win, wdw, wout):
    body = functools.partial(_sconv_kernel, chunks=_col_chunks(D_MODEL, 512))
    scratch = lambda nb, tt: [pltpu.VMEM((nb, SCONV_PAD + tt, D_MODEL), F32)]
    return _call(body, "short_conv", x, hist, [_row(ng), win, wdw, wout], scratch,
                 (SCONV_WIDTH - 1, D_MODEL))


def _hgrn(x, s0, ng, w4, wo, lb_logits, hng, layer):
    body = functools.partial(_hgrn_kernel, layer=layer)
    scratch = lambda nb, tt: [pltpu.VMEM((nb, HGRN_HEADS, HGRN_HEAD_DIM, HGRN_HEAD_DIM), F32),
                              pltpu.VMEM((nb * tt, D_MODEL), BF16)]
    return _call(body, "hgrn2", x, s0, [_row(ng), w4, wo, lb_logits, _row(hng)], scratch,
                 (HGRN_HEADS, HGRN_HEAD_DIM, HGRN_HEAD_DIM))


def _trunk(x, pos0, conf_hist, pool_hist, sconv_hist, hgrn_state, ffn_hist, p):
    depth = p['norm_mix'].shape[0]
    new = {'conf': [], 'pool': [], 'sconv': [], 'hgrn': [], 'ffn': []}
    for i in range(depth):
        mixer, j = i % 4, i // 4
        ng = p['norm_mix'][i]
        if mixer == 0:
            x, h = _conf(x, conf_hist[j], ng, p['conf_w_pw1'][j], p['conf_b_pw1'][j],
                         p['conf_w_dw'][j], p['conf_b_dw'][j], p['conf_ln_g'][j], p['conf_ln_b'][j],
                         p['conf_w_pw2'][j], p['conf_b_pw2'][j])
            new['conf'].append(h)
        elif mixer == 1:
            x, h = _pool(x, pool_hist[j], ng, p['pool_w'][j], p['pool_scale'][j], pos0)
            new['pool'].append(h)
        elif mixer == 2:
            x, h = _sconv(x, sconv_hist[j], ng, p['sconv_w_in'][j], p['sconv_w_dw'][j],
                          p['sconv_w_out'][j])
            new['sconv'].append(h)
        else:
            x, h = _hgrn(x, hgrn_state[j], ng, p['hgrn_w4'][j], p['hgrn_w_o'][j],
                         p['hgrn_lb_logits'], p['hgrn_norm_g'][j], i)
            new['hgrn'].append(h)
        x, h = _ffn(x, ffn_hist[i], p['norm_ffn'][i], p['ffn_w_gate'][i], p['ffn_w_up'][i],
                    p['ffn_w_dw'][i], p['ffn_b_dw'][i], p['ffn_w_down'][i], p['norm_final'],
                    final_norm=(i == depth - 1))
        new['ffn'].append(h)
    return (x,) + tuple(jnp.stack(new[k]) for k in ('conf', 'pool', 'sconv', 'hgrn', 'ffn'))


def kernel(x_prompt, x_sample, state_conformer_conv, state_pool, state_short_conv, state_hgrn, state_ffn_conv, norm_mix, norm_ffn, norm_final, conf_w_pw1, conf_b_pw1, conf_w_dw, conf_b_dw, conf_ln_g, conf_ln_b, conf_w_pw2, conf_b_pw2, pool_w, pool_scale, sconv_w_in, sconv_w_dw, sconv_w_out, hgrn_w_q, hgrn_w_f, hgrn_w_i, hgrn_w_g, hgrn_w_o, hgrn_lb_logits, hgrn_norm_g, ffn_w_gate, ffn_w_up, ffn_w_dw, ffn_b_dw, ffn_w_down):
    n_hgrn = hgrn_w_q.shape[0]
    w4 = jnp.stack([w.astype(BF16).reshape(n_hgrn, D_MODEL, HGRN_HEADS, HGRN_HEAD_DIM)
                    for w in (hgrn_w_q, hgrn_w_f, hgrn_w_i, hgrn_w_g)], axis=3)
    p = {
        'norm_mix': norm_mix, 'norm_ffn': norm_ffn, 'norm_final': norm_final,
        'conf_w_pw1': conf_w_pw1.astype(BF16), 'conf_b_pw1': conf_b_pw1, 'conf_w_dw': conf_w_dw,
        'conf_b_dw': conf_b_dw, 'conf_ln_g': conf_ln_g, 'conf_ln_b': conf_ln_b,
        'conf_w_pw2': conf_w_pw2.astype(BF16), 'conf_b_pw2': conf_b_pw2,
        'pool_w': pool_w.astype(BF16), 'pool_scale': pool_scale,
        'sconv_w_in': sconv_w_in.astype(BF16), 'sconv_w_dw': sconv_w_dw,
        'sconv_w_out': sconv_w_out.astype(BF16),
        'hgrn_w4': w4.reshape(n_hgrn, D_MODEL, HGRN_HEADS * HGRN_PROJ),
        'hgrn_w_o': hgrn_w_o.astype(BF16), 'hgrn_lb_logits': hgrn_lb_logits,
        'hgrn_norm_g': hgrn_norm_g,
        'ffn_w_gate': ffn_w_gate.astype(BF16), 'ffn_w_up': ffn_w_up.astype(BF16),
        'ffn_w_dw': ffn_w_dw, 'ffn_b_dw': ffn_b_dw, 'ffn_w_down': ffn_w_down.astype(BF16),
    }
    b = x_prompt.shape[0]
    zeros_like_state = lambda s: jnp.zeros((s.shape[0], b) + s.shape[2:], s.dtype)
    y_p, conf_p, pool_p, sconv_p, hgrn_p, ffn_p = _trunk(
        x_prompt, 0, zeros_like_state(state_conformer_conv), zeros_like_state(state_pool),
        zeros_like_state(state_short_conv), zeros_like_state(state_hgrn),
        zeros_like_state(state_ffn_conv), p)
    y_s, conf_s, pool_s, sconv_s, hgrn_s, ffn_s = _trunk(
        x_sample, PAST_LEN, state_conformer_conv, state_pool, state_short_conv, state_hgrn,
        state_ffn_conv, p)
    return (y_p, y_s, conf_p, conf_s, pool_p, pool_s, sconv_p, sconv_s,
            hgrn_p, hgrn_s, ffn_p, ffn_s)
```

```python
import functools

import jax
import jax.numpy as jnp
from jax import lax
from jax.experimental import pallas as pl
from jax.experimental.pallas import tpu as pltpu

D_MODEL = 1024
D_FF = 2816
CONF_WIDTH = 31
POOL_WINDOWS = (2, 4, 8, 16)
POOL_GROUP_DIM = D_MODEL // len(POOL_WINDOWS)
POOL_HIST = max(POOL_WINDOWS) - 1
SCONV_WIDTH = 3
FFN_WIDTH = 3
HGRN_HEAD_DIM = 128
HGRN_HEADS = D_MODEL // HGRN_HEAD_DIM
HGRN_CHUNK = 64
PAST_LEN = 2048
EPS = 1e-6

SUBLANES = 8
LANES = 128
TILE_ROWS = 512
VMEM_LIMIT_BYTES = 56 * 1024 * 1024

F32 = jnp.float32
BF16 = jnp.bfloat16


def _round_up(n, m):
    return (n + m - 1) // m * m


def _rms(x, g):
    return x * lax.rsqrt(jnp.mean(x * x, axis=-1, keepdims=True) + EPS) * g


def _dot(a, b):
    return jnp.dot(a, b, preferred_element_type=F32)


def _dot_nt(a, b):
    return lax.dot_general(a, b, (((1,), (1,)), ((), ())), preferred_element_type=F32)


def _dot_tn(a, b):
    return lax.dot_general(a, b, (((0,), (0,)), ((), ())), preferred_element_type=F32)


def _col_chunks(total, width):
    return [(lo, min(lo + width, total)) for lo in range(0, total, width)]


FFN_PAD = SUBLANES


def _ffn_kernel(x_ref, hist_ref, ng_ref, wg_ref, wu_ref, wdw_ref, bdw_ref, wd_ref, fin_ref,
                o_ref, nh_ref, cbuf, *, nb, tt, chunks, final_norm):
    m = nb * tt
    h0 = FFN_PAD - (FFN_WIDTH - 1)

    @pl.when(pl.program_id(1) == 0)
    def _():
        cbuf[:, h0:FFN_PAD, :] = hist_ref[...]

    x = x_ref[...].reshape(m, D_MODEL)
    xn = _rms(x, ng_ref[...]).astype(BF16)
    acc = x
    for lo, hi in chunks:
        nc = hi - lo
        g = _dot(xn, wg_ref[:, lo:hi]).reshape(nb, tt, nc)
        cbuf[:, FFN_PAD:FFN_PAD + tt, lo:hi] = g
        g1 = cbuf[:, FFN_PAD - 1:FFN_PAD - 1 + tt, lo:hi]
        g2 = cbuf[:, FFN_PAD - 2:FFN_PAD - 2 + tt, lo:hi]
        w = wdw_ref[:, lo:hi]
        c = g * w[2:3] + g1 * w[1:2] + g2 * w[0:1] + bdw_ref[:, lo:hi]
        cbuf[:, h0:FFN_PAD, lo:hi] = cbuf[:, h0 + tt:FFN_PAD + tt, lo:hi]
        u = _dot(xn, wu_ref[:, lo:hi])
        h = (c * jax.nn.sigmoid(c)).reshape(m, nc) * u
        acc = acc + _dot(h.astype(BF16), wd_ref[lo:hi, :])
    nh_ref[...] = cbuf[:, h0:FFN_PAD, :]
    if final_norm:
        acc = _rms(acc, fin_ref[...])
    o_ref[...] = acc.reshape(nb, tt, D_MODEL)


CONF_PAD = _round_up(CONF_WIDTH - 1, SUBLANES)
ROW_STRIDE = 4
ROW_GROUP = ROW_STRIDE * SUBLANES
N_SLABS = D_MODEL // LANES


def _conf_kernel(x_ref, hist_ref, ng_ref, w1_ref, b1_ref, wdw_ref, bdw_ref, lng_ref, lnb_ref,
                 w2_ref, b2_ref, o_ref, nh_ref, ubuf, cbuf, *, nb, tt, chunks):
    m = nb * tt
    hw = CONF_WIDTH - 1
    h0 = CONF_PAD - hw

    @pl.when(pl.program_id(1) == 0)
    def _():
        for j in range(N_SLABS):
            ubuf[j, :, h0:CONF_PAD, :] = hist_ref[:, :, j * LANES:(j + 1) * LANES]

    x = x_ref[...].reshape(m, D_MODEL)
    xn = _rms(x, ng_ref[...]).astype(BF16)
    for lo, hi in chunks:
        a = _dot(xn, w1_ref[:, lo:hi]) + b1_ref[:, lo:hi]
        gt = _dot(xn, w1_ref[:, D_MODEL + lo:D_MODEL + hi]) + b1_ref[:, D_MODEL + lo:D_MODEL + hi]
        u = a * jax.nn.sigmoid(gt)
        for l0 in range(lo, hi, LANES):
            ubuf[l0 // LANES, :, CONF_PAD:CONF_PAD + tt, :] = (
                u[:, l0 - lo:l0 - lo + LANES].reshape(nb, tt, LANES))

    def slab(j, carry):
        l0 = pl.multiple_of(j * LANES, LANES)
        w = wdw_ref[:, pl.ds(l0, LANES)]
        wk = [jnp.broadcast_to(w[k:k + 1], (SUBLANES, LANES)) for k in range(CONF_WIDTH)]
        bias = jnp.broadcast_to(bdw_ref[:, pl.ds(l0, LANES)], (SUBLANES, LANES))
        for b in range(nb):
            for r0 in range(0, tt, ROW_GROUP):
                for ph in range(ROW_STRIDE):
                    acc = bias
                    for k in range(CONF_WIDTH):
                        rows = pl.ds(h0 + k + r0 + ph, SUBLANES, stride=ROW_STRIDE)
                        acc = acc + ubuf[j, b, rows, :] * wk[k]
                    cbuf[j, b, pl.ds(r0 + ph, SUBLANES, stride=ROW_STRIDE), :] = acc
        return carry

    lax.fori_loop(0, N_SLABS, slab, 0)

    for j in range(N_SLABS):
        nh_ref[:, :, j * LANES:(j + 1) * LANES] = ubuf[j, :, h0 + tt:CONF_PAD + tt, :]
        ubuf[j, :, h0:CONF_PAD, :] = ubuf[j, :, h0 + tt:CONF_PAD + tt, :]

    c = jnp.concatenate([cbuf[j].reshape(m, LANES) for j in range(N_SLABS)], axis=-1)
    mu = jnp.mean(c, axis=-1, keepdims=True)
    cc = c - mu
    var = jnp.mean(cc * cc, axis=-1, keepdims=True)
    cn = cc * lax.rsqrt(var + EPS) * lng_ref[...] + lnb_ref[...]
    s = cn * jax.nn.sigmoid(cn)
    y = _dot(s.astype(BF16), w2_ref[...]) + b2_ref[...]
    o_ref[...] = (x + y).reshape(nb, tt, D_MODEL)


POOL_PAD = _round_up(POOL_HIST, SUBLANES)


def _pool_kernel(x_ref, hist_ref, ng_ref, wp_ref, sc_ref, o_ref, nh_ref, pbuf, dbuf,
                 *, nb, tt, pos0):
    m = nb * tt
    h0 = POOL_PAD - POOL_HIST
    t = pl.program_id(1)

    @pl.when(t == 0)
    def _():
        pbuf[:, h0:POOL_PAD, :] = hist_ref[...]

    x = x_ref[...].reshape(m, D_MODEL)
    xn = _rms(x, ng_ref[...])
    pbuf[:, POOL_PAD:POOL_PAD + tt, :] = xn.reshape(nb, tt, D_MODEL)

    pos = pos0 + t * tt + lax.broadcasted_iota(jnp.int32, (tt, 1), 0)
    for gi, win in enumerate(POOL_WINDOWS):
        inv = 1.0 / jnp.minimum(pos + 1, win).astype(F32)
        for l0 in range(gi * POOL_GROUP_DIM, (gi + 1) * POOL_GROUP_DIM, LANES):
            for b in range(nb):
                for r0 in range(0, tt, SUBLANES):
                    cur = pbuf[b, POOL_PAD + r0:POOL_PAD + r0 + SUBLANES, l0:l0 + LANES]
                    acc = cur
                    for i in range(1, win):
                        acc = acc + pbuf[b, POOL_PAD + r0 - i:POOL_PAD + r0 - i + SUBLANES, l0:l0 + LANES]
                    dbuf[b, r0:r0 + SUBLANES, l0:l0 + LANES] = acc * inv[r0:r0 + SUBLANES] - cur

    nh_ref[...] = pbuf[:, h0 + tt:POOL_PAD + tt, :]
    pbuf[:, h0:POOL_PAD, :] = pbuf[:, h0 + tt:POOL_PAD + tt, :]

    for gi in range(len(POOL_WINDOWS)):
        lo, hi = gi * POOL_GROUP_DIM, (gi + 1) * POOL_GROUP_DIM
        d = dbuf[:, :, lo:hi].reshape(m, POOL_GROUP_DIM).astype(BF16)
        y = _dot(d, wp_ref[gi]) * sc_ref[:, lo:hi]
        o_ref[:, :, lo:hi] = (x[:, lo:hi] + y).reshape(nb, tt, POOL_GROUP_DIM)


SCONV_PAD = SUBLANES


def _sconv_kernel(x_ref, hist_ref, ng_ref, win_ref, wdw_ref, wout_ref, o_ref, nh_ref, pbuf,
                  *, nb, tt, chunks):
    m = nb * tt
    h0 = SCONV_PAD - (SCONV_WIDTH - 1)

    @pl.when(pl.program_id(1) == 0)
    def _():
        pbuf[:, h0:SCONV_PAD, :] = hist_ref[...]

    x = x_ref[...].reshape(m, D_MODEL)
    xn = _rms(x, ng_ref[...]).astype(BF16)
    acc = x
    for lo, hi in chunks:
        nc = hi - lo
        bg = _dot(xn, win_ref[:, lo:hi])
        cg = _dot(xn, win_ref[:, D_MODEL + lo:D_MODEL + hi])
        v = _dot(xn, win_ref[:, 2 * D_MODEL + lo:2 * D_MODEL + hi])
        p = (cg * v).reshape(nb, tt, nc)
        pbuf[:, SCONV_PAD:SCONV_PAD + tt, lo:hi] = p
        p1 = pbuf[:, SCONV_PAD - 1:SCONV_PAD - 1 + tt, lo:hi]
        p2 = pbuf[:, SCONV_PAD - 2:SCONV_PAD - 2 + tt, lo:hi]
        w = wdw_ref[:, lo:hi]
        c = p * w[2:3] + p1 * w[1:2] + p2 * w[0:1]
        pbuf[:, h0:SCONV_PAD, lo:hi] = pbuf[:, h0 + tt:SCONV_PAD + tt, lo:hi]
        acc = acc + _dot((bg * c.reshape(m, nc)).astype(BF16), wout_ref[lo:hi, :])
    nh_ref[...] = pbuf[:, h0:SCONV_PAD, :]
    o_ref[...] = acc.reshape(nb, tt, D_MODEL)


HGRN_PROJ = 4 * HGRN_HEAD_DIM


def _hgrn_kernel(x_ref, s0_ref, ng_ref, w4_ref, wo_ref, lbl_ref, hng_ref, o_ref, sout_ref,
                 st_ref, onbuf, *, nb, tt, layer):
    m = nb * tt
    hd = HGRN_HEAD_DIM
    ck = HGRN_CHUNK
    t = pl.program_id(1)

    @pl.when(t == 0)
    def _():
        for b in range(nb):
            for h in range(HGRN_HEADS):
                st_ref[b, h] = s0_ref[b, h].T

    x = x_ref[...].reshape(m, D_MODEL)
    xn = _rms(x, ng_ref[...]).astype(BF16)

    lg = lbl_ref[...]
    e = jnp.exp(lg - jnp.max(lg, axis=0, keepdims=True))
    p = e / jnp.sum(e, axis=0, keepdims=True)
    lb_all = p[1:2]
    for i in range(2, layer + 1):
        lb_all = lb_all + p[i:i + 1]
    if layer == 0:
        lb_all = jnp.zeros_like(p[0:1])

    row = lax.broadcasted_iota(jnp.int32, (ck, ck), 0)
    col = lax.broadcasted_iota(jnp.int32, (ck, ck), 1)
    causal = row >= col
    tri = causal.astype(BF16)
    tri3 = jnp.concatenate([tri, tri, tri], axis=1)
    hng = hng_ref[...]

    for h in range(HGRN_HEADS):
        z = _dot(xn, w4_ref[:, h * HGRN_PROJ:(h + 1) * HGRN_PROJ])
        lb = lb_all[:, h * hd:(h + 1) * hd]
        zq = z[:, 0:hd]
        q = zq * jax.nn.sigmoid(zq)
        f = lb + (1.0 - lb) * jax.nn.sigmoid(z[:, hd:2 * hd])
        k = 1.0 - f
        lf = jnp.log(f)
        v = z[:, 2 * hd:3 * hd]
        zg = z[:, 3 * hd:4 * hd]
        gate = zg * jax.nn.sigmoid(zg)
        for b in range(nb):
            for c0 in range(0, tt, ck):
                r0 = b * tt + c0
                lf_c = lf[r0:r0 + ck]
                hi = lf_c.astype(BF16)
                r1 = lf_c - hi.astype(F32)
                mid = r1.astype(BF16)
                lo = (r1 - mid.astype(F32)).astype(BF16)
                bc = _dot(tri3, jnp.concatenate([hi, mid, lo], axis=0))
                bl = bc[ck - 1:ck]
                q_c = q[r0:r0 + ck]
                k_c = k[r0:r0 + ck]
                v_c = v[r0:r0 + ck].astype(BF16)
                qd = (q_c * jnp.exp(bc)).astype(BF16)
                kd = (k_c * jnp.exp(-bc)).astype(BF16)
                kk = (k_c * jnp.exp(bl - bc)).astype(BF16)
                sc = jnp.where(causal, _dot_nt(qd, kd), 0.0)
                st = st_ref[b, h]
                o = _dot(sc.astype(BF16), v_c) + _dot_nt(qd, st.astype(BF16))
                st_ref[b, h] = st * jnp.exp(bl) + _dot_tn(v_c, kk)
                on = o * lax.rsqrt(jnp.mean(o * o, axis=-1, keepdims=True) + EPS) * hng
                onbuf[r0:r0 + ck, h * hd:(h + 1) * hd] = (on * gate[r0:r0 + ck]).astype(BF16)

    y = _dot(onbuf[...], wo_ref[...])
    o_ref[...] = (x + y).reshape(nb, tt, D_MODEL)

    @pl.when(t == pl.num_programs(1) - 1)
    def _():
        for b in range(nb):
            for h in range(HGRN_HEADS):
                sout_ref[b, h] = st_ref[b, h].T


def _tiling(batch, seq):
    tt = min(seq, TILE_ROWS)
    nb = max(1, min(batch, TILE_ROWS // tt))
    assert seq % tt == 0 and batch % nb == 0 and tt % HGRN_CHUNK == 0
    return nb, tt


def _resident(shape):
    zeros = (0,) * len(shape)
    return pl.BlockSpec(shape, lambda b, t: zeros, pipeline_mode=pl.Buffered(1))


def _x_spec(nb, tt):
    return pl.BlockSpec((nb, tt, D_MODEL), lambda b, t: (b, t, 0))


def _state_spec(nb, shape):
    zeros = (0,) * len(shape)
    return pl.BlockSpec((nb,) + shape, lambda b, t: (b,) + zeros)


def _call(body, name, x, state, consts, scratch, extra_out_shape):
    batch, seq, _ = x.shape
    nb, tt = _tiling(batch, seq)
    state_shape = state.shape[1:]
    return pl.pallas_call(
        functools.partial(body, nb=nb, tt=tt),
        name=name,
        grid=(batch // nb, seq // tt),
        in_specs=[_x_spec(nb, tt), _state_spec(nb, state_shape)] + [_resident(c.shape) for c in consts],
        out_specs=[_x_spec(nb, tt), _state_spec(nb, extra_out_shape)],
        out_shape=[jax.ShapeDtypeStruct(x.shape, F32),
                   jax.ShapeDtypeStruct((batch,) + extra_out_shape, F32)],
        scratch_shapes=scratch(nb, tt),
        compiler_params=pltpu.CompilerParams(
            dimension_semantics=("arbitrary", "arbitrary"),
            vmem_limit_bytes=VMEM_LIMIT_BYTES),
    )(x, state, *consts)


def _row(v):
    return v.reshape(1, -1)


def _ffn(x, hist, ng, wg, wu, wdw, bdw, wd, fin, final_norm):
    body = functools.partial(_ffn_kernel, chunks=_col_chunks(D_FF, 512), final_norm=final_norm)
    scratch = lambda nb, tt: [pltpu.VMEM((nb, FFN_PAD + tt, D_FF), F32)]
    return _call(body, "conv_ffn", x, hist,
                 [_row(ng), wg, wu, wdw, _row(bdw), wd, _row(fin)], scratch, (FFN_WIDTH - 1, D_FF))


def _conf(x, hist, ng, w1, b1, wdw, bdw, lng, lnb, w2, b2):
    body = functools.partial(_conf_kernel, chunks=_col_chunks(D_MODEL, 512))
    scratch = lambda nb, tt: [pltpu.VMEM((N_SLABS, nb, CONF_PAD + tt, LANES), F32),
                              pltpu.VMEM((N_SLABS, nb, tt, LANES), F32)]
    return _call(body, "conformer_conv", x, hist,
                 [_row(ng), w1, _row(b1), wdw, _row(bdw), _row(lng), _row(lnb), w2, _row(b2)],
                 scratch, (CONF_WIDTH - 1, D_MODEL))


def _pool(x, hist, ng, wp, scale, pos0):
    body = functools.partial(_pool_kernel, pos0=pos0)
    scratch = lambda nb, tt: [pltpu.VMEM((nb, POOL_PAD + tt, D_MODEL), F32),
                              pltpu.VMEM((nb, tt, D_MODEL), F32)]
    return _call(body, "pool_mixer", x, hist, [_row(ng), wp, _row(scale)], scratch,
                 (POOL_HIST, D_MODEL))


def _sconv(x, hist, ng, win, wdw, wout):
    body = functools.partial(_sconv_kernel, chunks=_col_chunks(D_MODEL, 512))
    scratch = lambda nb, tt: [pltpu.VMEM((nb, SCONV_PAD + tt, D_MODEL), F32)]
    return _call(body, "short_conv", x, hist, [_row(ng), win, wdw, wout], scratch,
                 (SCONV_WIDTH - 1, D_MODEL))


def _hgrn(x, s0, ng, w4, wo, lb_logits, hng, layer):
    body = functools.partial(_hgrn_kernel, layer=layer)
    scratch = lambda nb, tt: [pltpu.VMEM((nb, HGRN_HEADS, HGRN_HEAD_DIM, HGRN_HEAD_DIM), F32),
                              pltpu.VMEM((nb * tt, D_MODEL), BF16)]
    return _call(body, "hgrn2", x, s0, [_row(ng), w4, wo, lb_logits, _row(hng)], scratch,
                 (HGRN_HEADS, HGRN_HEAD_DIM, HGRN_HEAD_DIM))


def _trunk(x, pos0, conf_hist, pool_hist, sconv_hist, hgrn_state, ffn_hist, p):
    depth = p['norm_mix'].shape[0]
    new = {'conf': [], 'pool': [], 'sconv': [], 'hgrn': [], 'ffn': []}
    for i in range(depth):
        mixer, j = i % 4, i // 4
        ng = p['norm_mix'][i]
        if mixer == 0:
            x, h = _conf(x, conf_hist[j], ng, p['conf_w_pw1'][j], p['conf_b_pw1'][j],
                         p['conf_w_dw'][j], p['conf_b_dw'][j], p['conf_ln_g'][j], p['conf_ln_b'][j],
                         p['conf_w_pw2'][j], p['conf_b_pw2'][j])
            new['conf'].append(h)
        elif mixer == 1:
            x, h = _pool(x, pool_hist[j], ng, p['pool_w'][j], p['pool_scale'][j], pos0)
            new['pool'].append(h)
        elif mixer == 2:
            x, h = _sconv(x, sconv_hist[j], ng, p['sconv_w_in'][j], p['sconv_w_dw'][j],
                          p['sconv_w_out'][j])
            new['sconv'].append(h)
        else:
            x, h = _hgrn(x, hgrn_state[j], ng, p['hgrn_w4'][j], p['hgrn_w_o'][j],
                         p['hgrn_lb_logits'], p['hgrn_norm_g'][j], i)
            new['hgrn'].append(h)
        x, h = _ffn(x, ffn_hist[i], p['norm_ffn'][i], p['ffn_w_gate'][i], p['ffn_w_up'][i],
                    p['ffn_w_dw'][i], p['ffn_b_dw'][i], p['ffn_w_down'][i], p['norm_final'],
                    final_norm=(i == depth - 1))
        new['ffn'].append(h)
    return (x,) + tuple(jnp.stack(new[k]) for k in ('conf', 'pool', 'sconv', 'hgrn', 'ffn'))


def kernel(x_prompt, x_sample, state_conformer_conv, state_pool, state_short_conv, state_hgrn, state_ffn_conv, norm_mix, norm_ffn, norm_final, conf_w_pw1, conf_b_pw1, conf_w_dw, conf_b_dw, conf_ln_g, conf_ln_b, conf_w_pw2, conf_b_pw2, pool_w, pool_scale, sconv_w_in, sconv_w_dw, sconv_w_out, hgrn_w_q, hgrn_w_f, hgrn_w_i, hgrn_w_g, hgrn_w_o, hgrn_lb_logits, hgrn_norm_g, ffn_w_gate, ffn_w_up, ffn_w_dw, ffn_b_dw, ffn_w_down):
    n_hgrn = hgrn_w_q.shape[0]
    w4 = jnp.stack([w.astype(BF16).reshape(n_hgrn, D_MODEL, HGRN_HEADS, HGRN_HEAD_DIM)
                    for w in (hgrn_w_q, hgrn_w_f, hgrn_w_i, hgrn_w_g)], axis=3)
    p = {
        'norm_mix': norm_mix, 'norm_ffn': norm_ffn, 'norm_final': norm_final,
        'conf_w_pw1': conf_w_pw1.astype(BF16), 'conf_b_pw1': conf_b_pw1, 'conf_w_dw': conf_w_dw,
        'conf_b_dw': conf_b_dw, 'conf_ln_g': conf_ln_g, 'conf_ln_b': conf_ln_b,
        'conf_w_pw2': conf_w_pw2.astype(BF16), 'conf_b_pw2': conf_b_pw2,
        'pool_w': pool_w.astype(BF16), 'pool_scale': pool_scale,
        'sconv_w_in': sconv_w_in.astype(BF16), 'sconv_w_dw': sconv_w_dw,
        'sconv_w_out': sconv_w_out.astype(BF16),
        'hgrn_w4': w4.reshape(n_hgrn, D_MODEL, HGRN_HEADS * HGRN_PROJ),
        'hgrn_w_o': hgrn_w_o.astype(BF16), 'hgrn_lb_logits': hgrn_lb_logits,
        'hgrn_norm_g': hgrn_norm_g,
        'ffn_w_gate': ffn_w_gate.astype(BF16), 'ffn_w_up': ffn_w_up.astype(BF16),
        'ffn_w_dw': ffn_w_dw, 'ffn_b_dw': ffn_b_dw, 'ffn_w_down': ffn_w_down.astype(BF16),
    }
    b = x_prompt.shape[0]
    zeros_like_state = lambda s: jnp.zeros((s.shape[0], b) + s.shape[2:], s.dtype)
    y_p, conf_p, pool_p, sconv_p, hgrn_p, ffn_p = _trunk(
        x_prompt, 0, zeros_like_state(state_conformer_conv), zeros_like_state(state_pool),
        zeros_like_state(state_short_conv), zeros_like_state(state_hgrn),
        zeros_like_state(state_ffn_conv), p)
    y_s, conf_s, pool_s, sconv_s, hgrn_s, ffn_s = _trunk(
        x_sample, PAST_LEN, state_conformer_conv, state_pool, state_short_conv, state_hgrn,
        state_ffn_conv, p)
    return (y_p, y_s, conf_p, conf_s, pool_p, pool_s, sconv_p, sconv_s,
            hgrn_p, hgrn_s, ffn_p, ffn_s)
```

```python
import functools

import jax
import jax.numpy as jnp
from jax import lax
from jax.experimental import pallas as pl
from jax.experimental.pallas import tpu as pltpu

D_MODEL = 1024
D_FF = 2816
CONF_WIDTH = 31
POOL_WINDOWS = (2, 4, 8, 16)
POOL_GROUP_DIM = D_MODEL // len(POOL_WINDOWS)
POOL_HIST = max(POOL_WINDOWS) - 1
SCONV_WIDTH = 3
FFN_WIDTH = 3
HGRN_HEAD_DIM = 128
HGRN_HEADS = D_MODEL // HGRN_HEAD_DIM
HGRN_CHUNK = 64
PAST_LEN = 2048
EPS = 1e-6

SUBLANES = 8
LANES = 128
TILE_ROWS = 512
VMEM_LIMIT_BYTES = 56 * 1024 * 1024

F32 = jnp.float32
BF16 = jnp.bfloat16


def _round_up(n, m):
    return (n + m - 1) // m * m


def _rms(x, g):
    return x * lax.rsqrt(jnp.mean(x * x, axis=-1, keepdims=True) + EPS) * g


def _dot(a, b):
    return jnp.dot(a, b, preferred_element_type=F32)


def _dot_nt(a, b):
    return lax.dot_general(a, b, (((1,), (1,)), ((), ())), preferred_element_type=F32)


def _dot_tn(a, b):
    return lax.dot_general(a, b, (((0,), (0,)), ((), ())), preferred_element_type=F32)


def _col_chunks(total, width):
    return [(lo, min(lo + width, total)) for lo in range(0, total, width)]


FFN_PAD = SUBLANES


def _ffn_kernel(x_ref, hist_ref, ng_ref, wg_ref, wu_ref, wdw_ref, bdw_ref, wd_ref, fin_ref,
                o_ref, nh_ref, cbuf, *, nb, tt, chunks, final_norm):
    m = nb * tt
    h0 = FFN_PAD - (FFN_WIDTH - 1)

    @pl.when(pl.program_id(1) == 0)
    def _():
        cbuf[:, h0:FFN_PAD, :] = hist_ref[...]

    x = x_ref[...].reshape(m, D_MODEL)
    xn = _rms(x, ng_ref[...]).astype(BF16)
    acc = x

    def up_projections(lo, hi):
        return _dot(xn, wg_ref[:, lo:hi]), _dot(xn, wu_ref[:, lo:hi])

    nxt = up_projections(*chunks[0])
    for ci, (lo, hi) in enumerate(chunks):
        nc = hi - lo
        g, u = nxt
        if ci + 1 < len(chunks):
            nxt = up_projections(*chunks[ci + 1])
        g = g.reshape(nb, tt, nc)
        cbuf[:, FFN_PAD:FFN_PAD + tt, lo:hi] = g
        g1 = cbuf[:, FFN_PAD - 1:FFN_PAD - 1 + tt, lo:hi]
        g2 = cbuf[:, FFN_PAD - 2:FFN_PAD - 2 + tt, lo:hi]
        w = wdw_ref[:, lo:hi]
        c = g * w[2:3] + g1 * w[1:2] + g2 * w[0:1] + bdw_ref[:, lo:hi]
        cbuf[:, h0:FFN_PAD, lo:hi] = cbuf[:, h0 + tt:FFN_PAD + tt, lo:hi]
        h = (c * jax.nn.sigmoid(c)).reshape(m, nc) * u
        acc = acc + _dot(h.astype(BF16), wd_ref[lo:hi, :])
    nh_ref[...] = cbuf[:, h0:FFN_PAD, :]
    if final_norm:
        acc = _rms(acc, fin_ref[...])
    o_ref[...] = acc.reshape(nb, tt, D_MODEL)


CONF_PAD = _round_up(CONF_WIDTH - 1, SUBLANES)
ROW_STRIDE = 4
ROW_GROUP = ROW_STRIDE * SUBLANES
N_SLABS = D_MODEL // LANES


def _conf_kernel(x_ref, hist_ref, ng_ref, w1_ref, b1_ref, wdw_ref, bdw_ref, lng_ref, lnb_ref,
                 w2_ref, b2_ref, o_ref, nh_ref, ubuf, cbuf, *, nb, tt, chunks):
    m = nb * tt
    hw = CONF_WIDTH - 1
    h0 = CONF_PAD - hw

    @pl.when(pl.program_id(1) == 0)
    def _():
        for j in range(N_SLABS):
            ubuf[j, :, h0:CONF_PAD, :] = hist_ref[:, :, j * LANES:(j + 1) * LANES]

    x = x_ref[...].reshape(m, D_MODEL)
    xn = _rms(x, ng_ref[...]).astype(BF16)
    for lo, hi in chunks:
        a = _dot(xn, w1_ref[:, lo:hi]) + b1_ref[:, lo:hi]
        gt = _dot(xn, w1_ref[:, D_MODEL + lo:D_MODEL + hi]) + b1_ref[:, D_MODEL + lo:D_MODEL + hi]
        u = a * jax.nn.sigmoid(gt)
        for l0 in range(lo, hi, LANES):
            ubuf[l0 // LANES, :, CONF_PAD:CONF_PAD + tt, :] = (
                u[:, l0 - lo:l0 - lo + LANES].reshape(nb, tt, LANES))

    def slab(j, carry):
        l0 = pl.multiple_of(j * LANES, LANES)
        w = wdw_ref[:, pl.ds(l0, LANES)]
        wk = [jnp.broadcast_to(w[k:k + 1], (SUBLANES, LANES)) for k in range(CONF_WIDTH)]
        bias = jnp.broadcast_to(bdw_ref[:, pl.ds(l0, LANES)], (SUBLANES, LANES))
        for b in range(nb):
            for r0 in range(0, tt, ROW_GROUP):
                for ph in range(ROW_STRIDE):
                    acc = bias
                    for k in range(CONF_WIDTH):
                        rows = pl.ds(h0 + k + r0 + ph, SUBLANES, stride=ROW_STRIDE)
                        acc = acc + ubuf[j, b, rows, :] * wk[k]
                    cbuf[j, b, pl.ds(r0 + ph, SUBLANES, stride=ROW_STRIDE), :] = acc
        return carry

    lax.fori_loop(0, N_SLABS, slab, 0)

    for j in range(N_SLABS):
        nh_ref[:, :, j * LANES:(j + 1) * LANES] = ubuf[j, :, h0 + tt:CONF_PAD + tt, :]
        ubuf[j, :, h0:CONF_PAD, :] = ubuf[j, :, h0 + tt:CONF_PAD + tt, :]

    c = jnp.concatenate([cbuf[j].reshape(m, LANES) for j in range(N_SLABS)], axis=-1)
    mu = jnp.mean(c, axis=-1, keepdims=True)
    cc = c - mu
    var = jnp.mean(cc * cc, axis=-1, keepdims=True)
    cn = cc * lax.rsqrt(var + EPS) * lng_ref[...] + lnb_ref[...]
    s = cn * jax.nn.sigmoid(cn)
    y = _dot(s.astype(BF16), w2_ref[...]) + b2_ref[...]
    o_ref[...] = (x + y).reshape(nb, tt, D_MODEL)


POOL_PAD = _round_up(POOL_HIST, SUBLANES)


def _pool_kernel(x_ref, hist_ref, ng_ref, wp_ref, sc_ref, o_ref, nh_ref, pbuf, dbuf,
                 *, nb, tt, pos0):
    m = nb * tt
    h0 = POOL_PAD - POOL_HIST
    t = pl.program_id(1)

    @pl.when(t == 0)
    def _():
        pbuf[:, h0:POOL_PAD, :] = hist_ref[...]

    x = x_ref[...].reshape(m, D_MODEL)
    xn = _rms(x, ng_ref[...])
    pbuf[:, POOL_PAD:POOL_PAD + tt, :] = xn.reshape(nb, tt, D_MODEL)

    pos = pos0 + t * tt + lax.broadcasted_iota(jnp.int32, (tt, 1), 0)
    for gi, win in enumerate(POOL_WINDOWS):
        inv = 1.0 / jnp.minimum(pos + 1, win).astype(F32)
        for l0 in range(gi * POOL_GROUP_DIM, (gi + 1) * POOL_GROUP_DIM, LANES):
            for b in range(nb):
                for r0 in range(0, tt, SUBLANES):
                    cur = pbuf[b, POOL_PAD + r0:POOL_PAD + r0 + SUBLANES, l0:l0 + LANES]
                    acc = cur
                    for i in range(1, win):
                        acc = acc + pbuf[b, POOL_PAD + r0 - i:POOL_PAD + r0 - i + SUBLANES, l0:l0 + LANES]
                    dbuf[b, r0:r0 + SUBLANES, l0:l0 + LANES] = acc * inv[r0:r0 + SUBLANES] - cur

    nh_ref[...] = pbuf[:, h0 + tt:POOL_PAD + tt, :]
    pbuf[:, h0:POOL_PAD, :] = pbuf[:, h0 + tt:POOL_PAD + tt, :]

    for gi in range(len(POOL_WINDOWS)):
        lo, hi = gi * POOL_GROUP_DIM, (gi + 1) * POOL_GROUP_DIM
        d = dbuf[:, :, lo:hi].reshape(m, POOL_GROUP_DIM).astype(BF16)
        y = _dot(d, wp_ref[gi]) * sc_ref[:, lo:hi]
        o_ref[:, :, lo:hi] = (x[:, lo:hi] + y).reshape(nb, tt, POOL_GROUP_DIM)


SCONV_PAD = SUBLANES


def _sconv_kernel(x_ref, hist_ref, ng_ref, win_ref, wdw_ref, wout_ref, o_ref, nh_ref, pbuf,
                  *, nb, tt, chunks):
    m = nb * tt
    h0 = SCONV_PAD - (SCONV_WIDTH - 1)

    @pl.when(pl.program_id(1) == 0)
    def _():
        pbuf[:, h0:SCONV_PAD, :] = hist_ref[...]

    x = x_ref[...].reshape(m, D_MODEL)
    xn = _rms(x, ng_ref[...]).astype(BF16)
    acc = x
    for lo, hi in chunks:
        nc = hi - lo
        bg = _dot(xn, win_ref[:, lo:hi])
        cg = _dot(xn, win_ref[:, D_MODEL + lo:D_MODEL + hi])
        v = _dot(xn, win_ref[:, 2 * D_MODEL + lo:2 * D_MODEL + hi])
        p = (cg * v).reshape(nb, tt, nc)
        pbuf[:, SCONV_PAD:SCONV_PAD + tt, lo:hi] = p
        p1 = pbuf[:, SCONV_PAD - 1:SCONV_PAD - 1 + tt, lo:hi]
        p2 = pbuf[:, SCONV_PAD - 2:SCONV_PAD - 2 + tt, lo:hi]
        w = wdw_ref[:, lo:hi]
        c = p * w[2:3] + p1 * w[1:2] + p2 * w[0:1]
        pbuf[:, h0:SCONV_PAD, lo:hi] = pbuf[:, h0 + tt:SCONV_PAD + tt, lo:hi]
        acc = acc + _dot((bg * c.reshape(m, nc)).astype(BF16), wout_ref[lo:hi, :])
    nh_ref[...] = pbuf[:, h0:SCONV_PAD, :]
    o_ref[...] = acc.reshape(nb, tt, D_MODEL)


HGRN_PROJ = 4 * HGRN_HEAD_DIM
HGRN_BLOCK = 256
HGRN_HEAD_GROUP = 8


def _hgrn_kernel(x_ref, s0_ref, ng_ref, w4_ref, wo_ref, lbl_ref, hng_ref, o_ref, sout_ref,
                 st_ref, onbuf, *, nb, tt, layer, group):
    m = nb * tt
    hd = HGRN_HEAD_DIM
    ck = HGRN_CHUNK
    t = pl.program_id(1)

    @pl.when(t == 0)
    def _():
        for b in range(nb):
            for h in range(HGRN_HEADS):
                st_ref[b, h] = s0_ref[b, h].T

    x = x_ref[...].reshape(m, D_MODEL)
    xn = _rms(x, ng_ref[...]).astype(BF16)

    lg = lbl_ref[...]
    e = jnp.exp(lg - jnp.max(lg, axis=0, keepdims=True))
    p = e / jnp.sum(e, axis=0, keepdims=True)
    lb_all = p[1:2]
    for i in range(2, layer + 1):
        lb_all = lb_all + p[i:i + 1]
    if layer == 0:
        lb_all = jnp.zeros_like(p[0:1])

    blk = HGRN_BLOCK
    row = lax.broadcasted_iota(jnp.int32, (blk, blk), 0)
    col = lax.broadcasted_iota(jnp.int32, (blk, blk), 1)
    ck_shift = ck.bit_length() - 1
    causal = (row >= col) & ((row >> ck_shift) == (col >> ck_shift))
    tri = causal.astype(BF16)
    hng = hng_ref[...]

    blocks = list(range(0, m, blk))
    chunk_starts = list(range(0, blk, ck))
    for g0 in range(0, HGRN_HEADS, group):
        heads = list(range(g0, g0 + group))
        q, k, v, gate, lf3 = {}, {}, {}, {}, {}
        for h in heads:
            z = _dot(xn, w4_ref[:, h * HGRN_PROJ:(h + 1) * HGRN_PROJ])
            lb = lb_all[:, h * hd:(h + 1) * hd]
            zq = z[:, 0:hd]
            q[h] = zq * jax.nn.sigmoid(zq)
            f = lb + (1.0 - lb) * jax.nn.sigmoid(z[:, hd:2 * hd])
            k[h] = 1.0 - f
            lf = jnp.log(f)
            v[h] = z[:, 2 * hd:3 * hd].astype(BF16)
            zg = z[:, 3 * hd:4 * hd]
            gate[h] = zg * jax.nn.sigmoid(zg)
            hi = lf.astype(BF16)
            r1 = lf - hi.astype(F32)
            mid = r1.astype(BF16)
            lo = (r1 - mid.astype(F32)).astype(BF16)
            lf3[h] = jnp.concatenate([hi, mid, lo], axis=1)
        units = [(h, r0) for h in heads for r0 in blocks]
        bc3 = {u: _dot(tri, lf3[u[0]][u[1]:u[1] + blk]) for u in units}
        qd, kd, kk, lasts = {}, {}, {}, {}
        for h, r0 in units:
            u = (h, r0)
            b3 = bc3[u]
            bc = b3[:, 0:hd] + b3[:, hd:2 * hd] + b3[:, 2 * hd:3 * hd]
            lasts[u] = [bc[c0 + ck - 1:c0 + ck] for c0 in chunk_starts]
            bl = jnp.concatenate([jnp.broadcast_to(l, (ck, hd)) for l in lasts[u]], axis=0)
            q_b = q[h][r0:r0 + blk]
            k_b = k[h][r0:r0 + blk]
            qd[u] = (q_b * jnp.exp(bc)).astype(BF16)
            kd[u] = (k_b * jnp.exp(-bc)).astype(BF16)
            kk[u] = (k_b * jnp.exp(bl - bc)).astype(BF16)
        sc = {u: jnp.where(causal, _dot_nt(qd[u], kd[u]), 0.0).astype(BF16) for u in units}
        o_intra = {u: _dot(sc[u], v[u[0]][u[1]:u[1] + blk]) for u in units}
        upd = {(h, r0, c0): _dot_tn(v[h][r0 + c0:r0 + c0 + ck], kk[(h, r0)][c0:c0 + ck])
               for h, r0 in units for c0 in chunk_starts}
        st_in = {}
        for h in heads:
            cur = {}
            for r0 in blocks:
                for ci, c0 in enumerate(chunk_starts):
                    b = (r0 + c0) // tt
                    if b not in cur:
                        cur[b] = st_ref[b, h]
                    st_in[(h, r0, c0)] = cur[b].astype(BF16)
                    cur[b] = cur[b] * jnp.exp(lasts[(h, r0)][ci]) + upd[(h, r0, c0)]
            for b, s in cur.items():
                st_ref[b, h] = s
        o_inter = {key: _dot_nt(qd[key[:2]][key[2]:key[2] + ck], st_in[key]) for key in st_in}
        for h, r0 in units:
            o = o_intra[(h, r0)] + jnp.concatenate([o_inter[(h, r0, c0)] for c0 in chunk_starts], axis=0)
            on = o * lax.rsqrt(jnp.mean(o * o, axis=-1, keepdims=True) + EPS) * hng
            onbuf[r0:r0 + blk, h * hd:(h + 1) * hd] = (on * gate[h][r0:r0 + blk]).astype(BF16)

    y = _dot(onbuf[...], wo_ref[...])
    o_ref[...] = (x + y).reshape(nb, tt, D_MODEL)

    @pl.when(t == pl.num_programs(1) - 1)
    def _():
        for b in range(nb):
            for h in range(HGRN_HEADS):
                sout_ref[b, h] = st_ref[b, h].T


def _tiling(batch, seq):
    tt = min(seq, TILE_ROWS)
    nb = max(1, min(batch, TILE_ROWS // tt))
    assert seq % tt == 0 and batch % nb == 0
    assert tt % HGRN_CHUNK == 0 and tt % ROW_GROUP == 0 and (nb * tt) % HGRN_BLOCK == 0
    return nb, tt


def _resident(shape):
    zeros = (0,) * len(shape)
    return pl.BlockSpec(shape, lambda b, t: zeros, pipeline_mode=pl.Buffered(1))


def _x_spec(nb, tt):
    return pl.BlockSpec((nb, tt, D_MODEL), lambda b, t: (b, t, 0))


def _state_spec(nb, shape):
    zeros = (0,) * len(shape)
    return pl.BlockSpec((nb,) + shape, lambda b, t: (b,) + zeros)


def _call(body, name, x, state, consts, scratch, extra_out_shape):
    batch, seq, _ = x.shape
    nb, tt = _tiling(batch, seq)
    state_shape = state.shape[1:]
    return pl.pallas_call(
        functools.partial(body, nb=nb, tt=tt),
        name=name,
        grid=(batch // nb, seq // tt),
        in_specs=[_x_spec(nb, tt), _state_spec(nb, state_shape)] + [_resident(c.shape) for c in consts],
        out_specs=[_x_spec(nb, tt), _state_spec(nb, extra_out_shape)],
        out_shape=[jax.ShapeDtypeStruct(x.shape, F32),
                   jax.ShapeDtypeStruct((batch,) + extra_out_shape, F32)],
        scratch_shapes=scratch(nb, tt),
        compiler_params=pltpu.CompilerParams(
            dimension_semantics=("arbitrary", "arbitrary"),
            vmem_limit_bytes=VMEM_LIMIT_BYTES),
    )(x, state, *consts)


def _row(v):
    return v.reshape(1, -1)


def _ffn(x, hist, ng, wg, wu, wdw, bdw, wd, fin, final_norm):
    body = functools.partial(_ffn_kernel, chunks=_col_chunks(D_FF, 512), final_norm=final_norm)
    scratch = lambda nb, tt: [pltpu.VMEM((nb, FFN_PAD + tt, D_FF), F32)]
    return _call(body, "conv_ffn", x, hist,
                 [_row(ng), wg, wu, wdw, _row(bdw), wd, _row(fin)], scratch, (FFN_WIDTH - 1, D_FF))


def _conf(x, hist, ng, w1, b1, wdw, bdw, lng, lnb, w2, b2):
    body = functools.partial(_conf_kernel, chunks=_col_chunks(D_MODEL, 512))
    scratch = lambda nb, tt: [pltpu.VMEM((N_SLABS, nb, CONF_PAD + tt, LANES), F32),
                              pltpu.VMEM((N_SLABS, nb, tt, LANES), F32)]
    return _call(body, "conformer_conv", x, hist,
                 [_row(ng), w1, _row(b1), wdw, _row(bdw), _row(lng), _row(lnb), w2, _row(b2)],
                 scratch, (CONF_WIDTH - 1, D_MODEL))


def _pool(x, hist, ng, wp, scale, pos0):
    body = functools.partial(_pool_kernel, pos0=pos0)
    scratch = lambda nb, tt: [pltpu.VMEM((nb, POOL_PAD + tt, D_MODEL), F32),
                              pltpu.VMEM((nb, tt, D_MODEL), F32)]
    return _call(body, "pool_mixer", x, hist, [_row(ng), wp, _row(scale)], scratch,
                 (POOL_HIST, D_MODEL))


def _sconv(x, hist, ng, win, wdw, wout):
    body = functools.partial(_sconv_kernel, chunks=_col_chunks(D_MODEL, 512))
    scratch = lambda nb, tt: [pltpu.VMEM((nb, SCONV_PAD + tt, D_MODEL), F32)]
    return _call(body, "short_conv", x, hist, [_row(ng), win, wdw, wout], scratch,
                 (SCONV_WIDTH - 1, D_MODEL))


def _hgrn(x, s0, ng, w4, wo, lb_logits, hng, layer):
    body = functools.partial(_hgrn_kernel, layer=layer, group=HGRN_HEAD_GROUP)
    scratch = lambda nb, tt: [pltpu.VMEM((nb, HGRN_HEADS, HGRN_HEAD_DIM, HGRN_HEAD_DIM), F32),
                              pltpu.VMEM((nb * tt, D_MODEL), BF16)]
    return _call(body, "hgrn2", x, s0, [_row(ng), w4, wo, lb_logits, _row(hng)], scratch,
                 (HGRN_HEADS, HGRN_HEAD_DIM, HGRN_HEAD_DIM))


def _trunk(x, pos0, conf_hist, pool_hist, sconv_hist, hgrn_state, ffn_hist, p):
    depth = p['norm_mix'].shape[0]
    new = {'conf': [], 'pool': [], 'sconv': [], 'hgrn': [], 'ffn': []}
    for i in range(depth):
        mixer, j = i % 4, i // 4
        ng = p['norm_mix'][i]
        if mixer == 0:
            x, h = _conf(x, conf_hist[j], ng, p['conf_w_pw1'][j], p['conf_b_pw1'][j],
                         p['conf_w_dw'][j], p['conf_b_dw'][j], p['conf_ln_g'][j], p['conf_ln_b'][j],
                         p['conf_w_pw2'][j], p['conf_b_pw2'][j])
            new['conf'].append(h)
        elif mixer == 1:
            x, h = _pool(x, pool_hist[j], ng, p['pool_w'][j], p['pool_scale'][j], pos0)
            new['pool'].append(h)
        elif mixer == 2:
            x, h = _sconv(x, sconv_hist[j], ng, p['sconv_w_in'][j], p['sconv_w_dw'][j],
                          p['sconv_w_out'][j])
            new['sconv'].append(h)
        else:
            x, h = _hgrn(x, hgrn_state[j], ng, p['hgrn_w4'][j], p['hgrn_w_o'][j],
                         p['hgrn_lb_logits'], p['hgrn_norm_g'][j], i)
            new['hgrn'].append(h)
        x, h = _ffn(x, ffn_hist[i], p['norm_ffn'][i], p['ffn_w_gate'][i], p['ffn_w_up'][i],
                    p['ffn_w_dw'][i], p['ffn_b_dw'][i], p['ffn_w_down'][i], p['norm_final'],
                    final_norm=(i == depth - 1))
        new['ffn'].append(h)
    return (x,) + tuple(jnp.stack(new[k]) for k in ('conf', 'pool', 'sconv', 'hgrn', 'ffn'))


def kernel(x_prompt, x_sample, state_conformer_conv, state_pool, state_short_conv, state_hgrn, state_ffn_conv, norm_mix, norm_ffn, norm_final, conf_w_pw1, conf_b_pw1, conf_w_dw, conf_b_dw, conf_ln_g, conf_ln_b, conf_w_pw2, conf_b_pw2, pool_w, pool_scale, sconv_w_in, sconv_w_dw, sconv_w_out, hgrn_w_q, hgrn_w_f, hgrn_w_i, hgrn_w_g, hgrn_w_o, hgrn_lb_logits, hgrn_norm_g, ffn_w_gate, ffn_w_up, ffn_w_dw, ffn_b_dw, ffn_w_down):
    n_hgrn = hgrn_w_q.shape[0]
    w4 = jnp.stack([w.astype(BF16).reshape(n_hgrn, D_MODEL, HGRN_HEADS, HGRN_HEAD_DIM)
                    for w in (hgrn_w_q, hgrn_w_f, hgrn_w_i, hgrn_w_g)], axis=3)
    p = {
        'norm_mix': norm_mix, 'norm_ffn': norm_ffn, 'norm_final': norm_final,
        'conf_w_pw1': conf_w_pw1.astype(BF16), 'conf_b_pw1': conf_b_pw1, 'conf_w_dw': conf_w_dw,
        'conf_b_dw': conf_b_dw, 'conf_ln_g': conf_ln_g, 'conf_ln_b': conf_ln_b,
        'conf_w_pw2': conf_w_pw2.astype(BF16), 'conf_b_pw2': conf_b_pw2,
        'pool_w': pool_w.astype(BF16), 'pool_scale': pool_scale,
        'sconv_w_in': sconv_w_in.astype(BF16), 'sconv_w_dw': sconv_w_dw,
        'sconv_w_out': sconv_w_out.astype(BF16),
        'hgrn_w4': w4.reshape(n_hgrn, D_MODEL, HGRN_HEADS * HGRN_PROJ),
        'hgrn_w_o': hgrn_w_o.astype(BF16), 'hgrn_lb_logits': hgrn_lb_logits,
        'hgrn_norm_g': hgrn_norm_g,
        'ffn_w_gate': ffn_w_gate.astype(BF16), 'ffn_w_up': ffn_w_up.astype(BF16),
        'ffn_w_dw': ffn_w_dw, 'ffn_b_dw': ffn_b_dw, 'ffn_w_down': ffn_w_down.astype(BF16),
    }
    b = x_prompt.shape[0]
    zeros_like_state = lambda s: jnp.zeros((s.shape[0], b) + s.shape[2:], s.dtype)
    y_p, conf_p, pool_p, sconv_p, hgrn_p, ffn_p = _trunk(
        x_prompt, 0, zeros_like_state(state_conformer_conv), zeros_like_state(state_pool),
        zeros_like_state(state_short_conv), zeros_like_state(state_hgrn),
        zeros_like_state(state_ffn_conv), p)
    y_s, conf_s, pool_s, sconv_s, hgrn_s, ffn_s = _trunk(
        x_sample, PAST_LEN, state_conformer_conv, state_pool, state_short_conv, state_hgrn,
        state_ffn_conv, p)
    return (y_p, y_s, conf_p, conf_s, pool_p, pool_s, sconv_p, sconv_s,
            hgrn_p, hgrn_s, ffn_p, ffn_s)
```

```python
import functools
from typing import NamedTuple

import jax
import jax.numpy as jnp
from jax import lax
from jax.experimental import pallas as pl
from jax.experimental.pallas import tpu as pltpu

D_MODEL = 1024
D_FF = 2816
CONF_WIDTH = 31
POOL_WINDOWS = (2, 4, 8, 16)
POOL_GROUP_DIM = D_MODEL // len(POOL_WINDOWS)
POOL_HIST = max(POOL_WINDOWS) - 1
SCONV_WIDTH = 3
FFN_WIDTH = 3
HGRN_HEAD_DIM = 128
HGRN_HEADS = D_MODEL // HGRN_HEAD_DIM
HGRN_CHUNK = 64
PAST_LEN = 2048
EPS = 1e-6

SUBLANES = 8
LANES = 128
TILE_ROWS = 512
VMEM_LIMIT_BYTES = 56 * 1024 * 1024

F32 = jnp.float32
BF16 = jnp.bfloat16


def _round_up(n, m):
    return (n + m - 1) // m * m


def _rms(x, g):
    return x * lax.rsqrt(jnp.mean(x * x, axis=-1, keepdims=True) + EPS) * g


def _dot(a, b):
    return jnp.dot(a, b, preferred_element_type=F32)


def _dot_nt(a, b):
    return lax.dot_general(a, b, (((1,), (1,)), ((), ())), preferred_element_type=F32)


def _dot_tn(a, b):
    return lax.dot_general(a, b, (((0,), (0,)), ((), ())), preferred_element_type=F32)


def _col_chunks(total, width):
    return [(lo, min(lo + width, total)) for lo in range(0, total, width)]


FFN_PAD = SUBLANES
FFN_TILE_ROWS = 1024
FFN_CHUNK = 256


def _ffn_kernel(x_ref, hist_ref, ng_ref, wg_ref, wu_ref, wdw_ref, bdw_ref, wd_ref, fin_ref,
                o_ref, nh_ref, cbuf, carry, *, nb, tt, chunks, final_norm):
    m = nb * tt
    h0 = FFN_PAD - (FFN_WIDTH - 1)

    @pl.when(pl.program_id(1) == 0)
    def _():
        carry[:, h0:FFN_PAD, :] = hist_ref[...]

    x = x_ref[...].reshape(m, D_MODEL)
    xn = _rms(x, ng_ref[...]).astype(BF16)
    acc = x

    def up_projections(lo, hi):
        return _dot(xn, wg_ref[:, lo:hi]), _dot(xn, wu_ref[:, lo:hi])

    nxt = up_projections(*chunks[0])
    for ci, (lo, hi) in enumerate(chunks):
        nc = hi - lo
        g, u = nxt
        if ci + 1 < len(chunks):
            nxt = up_projections(*chunks[ci + 1])
        g = g.reshape(nb, tt, nc)
        cb = cbuf.at[ci % 2]
        cb[:, h0:FFN_PAD, 0:nc] = carry[:, h0:FFN_PAD, lo:hi]
        cb[:, FFN_PAD:FFN_PAD + tt, 0:nc] = g
        g1 = cb[:, FFN_PAD - 1:FFN_PAD - 1 + tt, 0:nc]
        g2 = cb[:, FFN_PAD - 2:FFN_PAD - 2 + tt, 0:nc]
        w = wdw_ref[:, lo:hi]
        c = g * w[2:3] + g1 * w[1:2] + g2 * w[0:1] + bdw_ref[:, lo:hi]
        carry[:, h0:FFN_PAD, lo:hi] = cb[:, h0 + tt:FFN_PAD + tt, 0:nc]
        h = (c * jax.nn.sigmoid(c)).reshape(m, nc) * u
        acc = acc + _dot(h.astype(BF16), wd_ref[lo:hi, :])
    nh_ref[...] = carry[:, h0:FFN_PAD, :]
    if final_norm:
        acc = _rms(acc, fin_ref[...])
    o_ref[...] = acc.reshape(nb, tt, D_MODEL)


CONF_PAD = _round_up(CONF_WIDTH - 1, SUBLANES)
ROW_STRIDE = 4
ROW_GROUP = ROW_STRIDE * SUBLANES
N_SLABS = D_MODEL // LANES


def _conf_kernel(x_ref, hist_ref, ng_ref, w1_ref, b1_ref, wdw_ref, bdw_ref, lng_ref, lnb_ref,
                 w2_ref, b2_ref, o_ref, nh_ref, ubuf, cbuf, *, nb, tt, chunks):
    m = nb * tt
    hw = CONF_WIDTH - 1
    h0 = CONF_PAD - hw

    @pl.when(pl.program_id(1) == 0)
    def _():
        for j in range(N_SLABS):
            ubuf[j, :, h0:CONF_PAD, :] = hist_ref[:, :, j * LANES:(j + 1) * LANES]

    x = x_ref[...].reshape(m, D_MODEL)
    xn = _rms(x, ng_ref[...]).astype(BF16)
    for lo, hi in chunks:
        a = _dot(xn, w1_ref[:, lo:hi]) + b1_ref[:, lo:hi]
        gt = _dot(xn, w1_ref[:, D_MODEL + lo:D_MODEL + hi]) + b1_ref[:, D_MODEL + lo:D_MODEL + hi]
        u = a * jax.nn.sigmoid(gt)
        for l0 in range(lo, hi, LANES):
            ubuf[l0 // LANES, :, CONF_PAD:CONF_PAD + tt, :] = (
                u[:, l0 - lo:l0 - lo + LANES].reshape(nb, tt, LANES))

    def slab(j, carry):
        l0 = pl.multiple_of(j * LANES, LANES)
        w = wdw_ref[:, pl.ds(l0, LANES)]
        wk = [jnp.broadcast_to(w[k:k + 1], (SUBLANES, LANES)) for k in range(CONF_WIDTH)]
        bias = jnp.broadcast_to(bdw_ref[:, pl.ds(l0, LANES)], (SUBLANES, LANES))
        for b in range(nb):
            for r0 in range(0, tt, ROW_GROUP):
                for ph in range(ROW_STRIDE):
                    acc = bias
                    for k in range(CONF_WIDTH):
                        rows = pl.ds(h0 + k + r0 + ph, SUBLANES, stride=ROW_STRIDE)
                        acc = acc + ubuf[j, b, rows, :] * wk[k]
                    cbuf[j, b, pl.ds(r0 + ph, SUBLANES, stride=ROW_STRIDE), :] = acc
        return carry

    lax.fori_loop(0, N_SLABS, slab, 0)

    for j in range(N_SLABS):
        nh_ref[:, :, j * LANES:(j + 1) * LANES] = ubuf[j, :, h0 + tt:CONF_PAD + tt, :]
        ubuf[j, :, h0:CONF_PAD, :] = ubuf[j, :, h0 + tt:CONF_PAD + tt, :]

    c = jnp.concatenate([cbuf[j].reshape(m, LANES) for j in range(N_SLABS)], axis=-1)
    mu = jnp.mean(c, axis=-1, keepdims=True)
    cc = c - mu
    var = jnp.mean(cc * cc, axis=-1, keepdims=True)
    cn = cc * lax.rsqrt(var + EPS) * lng_ref[...] + lnb_ref[...]
    s = cn * jax.nn.sigmoid(cn)
    y = _dot(s.astype(BF16), w2_ref[...]) + b2_ref[...]
    o_ref[...] = (x + y).reshape(nb, tt, D_MODEL)


POOL_PAD = _round_up(POOL_HIST, SUBLANES)


def _pool_kernel(x_ref, hist_ref, ng_ref, wp_ref, sc_ref, o_ref, nh_ref, pbuf, dbuf,
                 *, nb, tt, pos0):
    m = nb * tt
    h0 = POOL_PAD - POOL_HIST
    t = pl.program_id(1)

    @pl.when(t == 0)
    def _():
        pbuf[:, h0:POOL_PAD, :] = hist_ref[...]

    x = x_ref[...].reshape(m, D_MODEL)
    xn = _rms(x, ng_ref[...])
    pbuf[:, POOL_PAD:POOL_PAD + tt, :] = xn.reshape(nb, tt, D_MODEL)

    pos = pos0 + t * tt + lax.broadcasted_iota(jnp.int32, (tt, 1), 0)
    for gi, win in enumerate(POOL_WINDOWS):
        inv = 1.0 / jnp.minimum(pos + 1, win).astype(F32)
        for l0 in range(gi * POOL_GROUP_DIM, (gi + 1) * POOL_GROUP_DIM, LANES):
            for b in range(nb):
                for r0 in range(0, tt, SUBLANES):
                    cur = pbuf[b, POOL_PAD + r0:POOL_PAD + r0 + SUBLANES, l0:l0 + LANES]
                    acc = cur
                    for i in range(1, win):
                        acc = acc + pbuf[b, POOL_PAD + r0 - i:POOL_PAD + r0 - i + SUBLANES, l0:l0 + LANES]
                    dbuf[b, r0:r0 + SUBLANES, l0:l0 + LANES] = acc * inv[r0:r0 + SUBLANES] - cur

    nh_ref[...] = pbuf[:, h0 + tt:POOL_PAD + tt, :]
    pbuf[:, h0:POOL_PAD, :] = pbuf[:, h0 + tt:POOL_PAD + tt, :]

    for gi in range(len(POOL_WINDOWS)):
        lo, hi = gi * POOL_GROUP_DIM, (gi + 1) * POOL_GROUP_DIM
        d = dbuf[:, :, lo:hi].reshape(m, POOL_GROUP_DIM).astype(BF16)
        y = _dot(d, wp_ref[gi]) * sc_ref[:, lo:hi]
        o_ref[:, :, lo:hi] = (x[:, lo:hi] + y).reshape(nb, tt, POOL_GROUP_DIM)


SCONV_PAD = SUBLANES


def _sconv_kernel(x_ref, hist_ref, ng_ref, win_ref, wdw_ref, wout_ref, o_ref, nh_ref, pbuf,
                  *, nb, tt, chunks):
    m = nb * tt
    h0 = SCONV_PAD - (SCONV_WIDTH - 1)

    @pl.when(pl.program_id(1) == 0)
    def _():
        pbuf[:, h0:SCONV_PAD, :] = hist_ref[...]

    x = x_ref[...].reshape(m, D_MODEL)
    xn = _rms(x, ng_ref[...]).astype(BF16)
    acc = x
    for lo, hi in chunks:
        nc = hi - lo
        bg = _dot(xn, win_ref[:, lo:hi])
        cg = _dot(xn, win_ref[:, D_MODEL + lo:D_MODEL + hi])
        v = _dot(xn, win_ref[:, 2 * D_MODEL + lo:2 * D_MODEL + hi])
        p = (cg * v).reshape(nb, tt, nc)
        pbuf[:, SCONV_PAD:SCONV_PAD + tt, lo:hi] = p
        p1 = pbuf[:, SCONV_PAD - 1:SCONV_PAD - 1 + tt, lo:hi]
        p2 = pbuf[:, SCONV_PAD - 2:SCONV_PAD - 2 + tt, lo:hi]
        w = wdw_ref[:, lo:hi]
        c = p * w[2:3] + p1 * w[1:2] + p2 * w[0:1]
        pbuf[:, h0:SCONV_PAD, lo:hi] = pbuf[:, h0 + tt:SCONV_PAD + tt, lo:hi]
        acc = acc + _dot((bg * c.reshape(m, nc)).astype(BF16), wout_ref[lo:hi, :])
    nh_ref[...] = pbuf[:, h0:SCONV_PAD, :]
    o_ref[...] = acc.reshape(nb, tt, D_MODEL)


HGRN_PROJ = 4 * HGRN_HEAD_DIM
HGRN_BLOCK = 256
HGRN_HEAD_GROUP = 8


def _hgrn_kernel(x_ref, s0_ref, ng_ref, w4_ref, wo_ref, lbl_ref, hng_ref, o_ref, sout_ref,
                 st_ref, onbuf, *, nb, tt, layer, group):
    m = nb * tt
    hd = HGRN_HEAD_DIM
    ck = HGRN_CHUNK
    t = pl.program_id(1)

    @pl.when(t == 0)
    def _():
        for b in range(nb):
            for h in range(HGRN_HEADS):
                st_ref[b, h] = s0_ref[b, h].T

    x = x_ref[...].reshape(m, D_MODEL)
    xn = _rms(x, ng_ref[...]).astype(BF16)

    lg = lbl_ref[...]
    e = jnp.exp(lg - jnp.max(lg, axis=0, keepdims=True))
    p = e / jnp.sum(e, axis=0, keepdims=True)
    lb_all = p[1:2]
    for i in range(2, layer + 1):
        lb_all = lb_all + p[i:i + 1]
    if layer == 0:
        lb_all = jnp.zeros_like(p[0:1])

    blk = HGRN_BLOCK
    row = lax.broadcasted_iota(jnp.int32, (blk, blk), 0)
    col = lax.broadcasted_iota(jnp.int32, (blk, blk), 1)
    ck_shift = ck.bit_length() - 1
    causal = (row >= col) & ((row >> ck_shift) == (col >> ck_shift))
    tri = causal.astype(BF16)
    hng = hng_ref[...]

    blocks = list(range(0, m, blk))
    chunk_starts = list(range(0, blk, ck))
    for g0 in range(0, HGRN_HEADS, group):
        heads = list(range(g0, g0 + group))
        q, k, v, gate, lf3 = {}, {}, {}, {}, {}
        for h in heads:
            z = _dot(xn, w4_ref[:, h * HGRN_PROJ:(h + 1) * HGRN_PROJ])
            lb = lb_all[:, h * hd:(h + 1) * hd]
            zq = z[:, 0:hd]
            q[h] = zq * jax.nn.sigmoid(zq)
            f = lb + (1.0 - lb) * jax.nn.sigmoid(z[:, hd:2 * hd])
            k[h] = 1.0 - f
            lf = jnp.log(f)
            v[h] = z[:, 2 * hd:3 * hd].astype(BF16)
            zg = z[:, 3 * hd:4 * hd]
            gate[h] = zg * jax.nn.sigmoid(zg)
            hi = lf.astype(BF16)
            r1 = lf - hi.astype(F32)
            mid = r1.astype(BF16)
            lo = (r1 - mid.astype(F32)).astype(BF16)
            lf3[h] = jnp.concatenate([hi, mid, lo], axis=1)
        units = [(h, r0) for h in heads for r0 in blocks]
        bc3 = {u: _dot(tri, lf3[u[0]][u[1]:u[1] + blk]) for u in units}
        qd, kd, kk, lasts = {}, {}, {}, {}
        for h, r0 in units:
            u = (h, r0)
            b3 = bc3[u]
            bc = b3[:, 0:hd] + b3[:, hd:2 * hd] + b3[:, 2 * hd:3 * hd]
            lasts[u] = [bc[c0 + ck - 1:c0 + ck] for c0 in chunk_starts]
            bl = jnp.concatenate([jnp.broadcast_to(l, (ck, hd)) for l in lasts[u]], axis=0)
            q_b = q[h][r0:r0 + blk]
            k_b = k[h][r0:r0 + blk]
            qd[u] = (q_b * jnp.exp(bc)).astype(BF16)
            kd[u] = (k_b * jnp.exp(-bc)).astype(BF16)
            kk[u] = (k_b * jnp.exp(bl - bc)).astype(BF16)
        sc = {u: jnp.where(causal, _dot_nt(qd[u], kd[u]), 0.0).astype(BF16) for u in units}
        o_intra = {u: _dot(sc[u], v[u[0]][u[1]:u[1] + blk]) for u in units}
        upd = {(h, r0, c0): _dot_tn(v[h][r0 + c0:r0 + c0 + ck], kk[(h, r0)][c0:c0 + ck])
               for h, r0 in units for c0 in chunk_starts}
        st_in = {}
        for h in heads:
            cur = {}
            for r0 in blocks:
                for ci, c0 in enumerate(chunk_starts):
                    b = (r0 + c0) // tt
                    if b not in cur:
                        cur[b] = st_ref[b, h]
                    st_in[(h, r0, c0)] = cur[b].astype(BF16)
                    cur[b] = cur[b] * jnp.exp(lasts[(h, r0)][ci]) + upd[(h, r0, c0)]
            for b, s in cur.items():
                st_ref[b, h] = s
        o_inter = {key: _dot_nt(qd[key[:2]][key[2]:key[2] + ck], st_in[key]) for key in st_in}
        for h, r0 in units:
            o = o_intra[(h, r0)] + jnp.concatenate([o_inter[(h, r0, c0)] for c0 in chunk_starts], axis=0)
            on = o * lax.rsqrt(jnp.mean(o * o, axis=-1, keepdims=True) + EPS) * hng
            onbuf[r0:r0 + blk, h * hd:(h + 1) * hd] = (on * gate[h][r0:r0 + blk]).astype(BF16)

    y = _dot(onbuf[...], wo_ref[...])
    o_ref[...] = (x + y).reshape(nb, tt, D_MODEL)

    @pl.when(t == pl.num_programs(1) - 1)
    def _():
        for b in range(nb):
            for h in range(HGRN_HEADS):
                sout_ref[b, h] = st_ref[b, h].T


def _tiling(batch, seq, rows):
    tt = min(seq, rows)
    nb = max(1, min(batch, rows // tt))
    assert seq % tt == 0 and batch % nb == 0
    assert tt % HGRN_CHUNK == 0 and tt % ROW_GROUP == 0 and (nb * tt) % HGRN_BLOCK == 0
    return nb, tt


class _Layer(NamedTuple):
    stacked: jax.Array
    index: int


def _operand(c):
    return c.stacked if isinstance(c, _Layer) else c


def _resident(c):
    if isinstance(c, _Layer):
        shape = c.stacked.shape[1:]
        index = (c.index,) + (0,) * len(shape)
        return pl.BlockSpec((None,) + shape, lambda b, t: index, pipeline_mode=pl.Buffered(1))
    zeros = (0,) * c.ndim
    return pl.BlockSpec(c.shape, lambda b, t: zeros, pipeline_mode=pl.Buffered(1))


def _x_spec(nb, tt):
    return pl.BlockSpec((nb, tt, D_MODEL), lambda b, t: (b, t, 0))


def _state_spec(nb, shape):
    zeros = (0,) * len(shape)
    return pl.BlockSpec((nb,) + shape, lambda b, t: (b,) + zeros)


def _call(body, name, x, state, consts, scratch, extra_out_shape, rows=TILE_ROWS):
    batch, seq, _ = x.shape
    nb, tt = _tiling(batch, seq, rows)
    state_shape = state.shape[1:]
    return pl.pallas_call(
        functools.partial(body, nb=nb, tt=tt),
        name=name,
        grid=(batch // nb, seq // tt),
        in_specs=[_x_spec(nb, tt), _state_spec(nb, state_shape)] + [_resident(c) for c in consts],
        out_specs=[_x_spec(nb, tt), _state_spec(nb, extra_out_shape)],
        out_shape=[jax.ShapeDtypeStruct(x.shape, F32),
                   jax.ShapeDtypeStruct((batch,) + extra_out_shape, F32)],
        scratch_shapes=scratch(nb, tt),
        compiler_params=pltpu.CompilerParams(
            dimension_semantics=("arbitrary", "arbitrary"),
            vmem_limit_bytes=VMEM_LIMIT_BYTES),
    )(x, state, *[_operand(c) for c in consts])


def _row(v):
    return v.reshape(1, -1)


def _ffn(x, hist, ng, wg, wu, wdw, bdw, wd, fin, final_norm):
    body = functools.partial(_ffn_kernel, chunks=_col_chunks(D_FF, FFN_CHUNK), final_norm=final_norm)
    scratch = lambda nb, tt: [pltpu.VMEM((2, nb, FFN_PAD + tt, FFN_CHUNK), F32),
                              pltpu.VMEM((nb, FFN_PAD, D_FF), F32)]
    return _call(body, "conv_ffn", x, hist,
                 [_row(ng), wg, wu, wdw, _row(bdw), wd, _row(fin)], scratch, (FFN_WIDTH - 1, D_FF),
                 rows=FFN_TILE_ROWS)


def _conf(x, hist, ng, w1, b1, wdw, bdw, lng, lnb, w2, b2):
    body = functools.partial(_conf_kernel, chunks=_col_chunks(D_MODEL, 512))
    scratch = lambda nb, tt: [pltpu.VMEM((N_SLABS, nb, CONF_PAD + tt, LANES), F32),
                              pltpu.VMEM((N_SLABS, nb, tt, LANES), F32)]
    return _call(body, "conformer_conv", x, hist,
                 [_row(ng), w1, _row(b1), wdw, _row(bdw), _row(lng), _row(lnb), w2, _row(b2)],
                 scratch, (CONF_WIDTH - 1, D_MODEL))


def _pool(x, hist, ng, wp, scale, pos0):
    body = functools.partial(_pool_kernel, pos0=pos0)
    scratch = lambda nb, tt: [pltpu.VMEM((nb, POOL_PAD + tt, D_MODEL), F32),
                              pltpu.VMEM((nb, tt, D_MODEL), F32)]
    return _call(body, "pool_mixer", x, hist, [_row(ng), wp, _row(scale)], scratch,
                 (POOL_HIST, D_MODEL))


def _sconv(x, hist, ng, win, wdw, wout):
    body = functools.partial(_sconv_kernel, chunks=_col_chunks(D_MODEL, 512))
    scratch = lambda nb, tt: [pltpu.VMEM((nb, SCONV_PAD + tt, D_MODEL), F32)]
    return _call(body, "short_conv", x, hist, [_row(ng), win, wdw, wout], scratch,
                 (SCONV_WIDTH - 1, D_MODEL))


def _hgrn(x, s0, ng, w4, wo, lb_logits, hng, layer):
    body = functools.partial(_hgrn_kernel, layer=layer, group=HGRN_HEAD_GROUP)
    scratch = lambda nb, tt: [pltpu.VMEM((nb, HGRN_HEADS, HGRN_HEAD_DIM, HGRN_HEAD_DIM), F32),
                              pltpu.VMEM((nb * tt, D_MODEL), BF16)]
    return _call(body, "hgrn2", x, s0, [_row(ng), w4, wo, lb_logits, _row(hng)], scratch,
                 (HGRN_HEADS, HGRN_HEAD_DIM, HGRN_HEAD_DIM))


def _trunk(x, pos0, conf_hist, pool_hist, sconv_hist, hgrn_state, ffn_hist, p):
    depth = p['norm_mix'].shape[0]
    new = {'conf': [], 'pool': [], 'sconv': [], 'hgrn': [], 'ffn': []}
    for i in range(depth):
        mixer, j = i % 4, i // 4
        ng = p['norm_mix'][i]
        if mixer == 0:
            x, h = _conf(x, conf_hist[j], ng, p['conf_w_pw1'][j], p['conf_b_pw1'][j],
                         p['conf_w_dw'][j], p['conf_b_dw'][j], p['conf_ln_g'][j], p['conf_ln_b'][j],
                         p['conf_w_pw2'][j], p['conf_b_pw2'][j])
            new['conf'].append(h)
        elif mixer == 1:
            x, h = _pool(x, pool_hist[j], ng, p['pool_w'][j], p['pool_scale'][j], pos0)
            new['pool'].append(h)
        elif mixer == 2:
            x, h = _sconv(x, sconv_hist[j], ng, p['sconv_w_in'][j], p['sconv_w_dw'][j],
                          p['sconv_w_out'][j])
            new['sconv'].append(h)
        else:
            x, h = _hgrn(x, hgrn_state[j], ng, p['hgrn_w4'][j], p['hgrn_w_o'][j],
                         p['hgrn_lb_logits'], p['hgrn_norm_g'][j], i)
            new['hgrn'].append(h)
        x, h = _ffn(x, ffn_hist[i], p['norm_ffn'][i], _Layer(p['ffn_w_gate'], i),
                    _Layer(p['ffn_w_up'], i), p['ffn_w_dw'][i], p['ffn_b_dw'][i],
                    _Layer(p['ffn_w_down'], i), p['norm_final'],
                    final_norm=(i == depth - 1))
        new['ffn'].append(h)
    return (x,) + tuple(jnp.stack(new[k]) for k in ('conf', 'pool', 'sconv', 'hgrn', 'ffn'))


def kernel(x_prompt, x_sample, state_conformer_conv, state_pool, state_short_conv, state_hgrn, state_ffn_conv, norm_mix, norm_ffn, norm_final, conf_w_pw1, conf_b_pw1, conf_w_dw, conf_b_dw, conf_ln_g, conf_ln_b, conf_w_pw2, conf_b_pw2, pool_w, pool_scale, sconv_w_in, sconv_w_dw, sconv_w_out, hgrn_w_q, hgrn_w_f, hgrn_w_i, hgrn_w_g, hgrn_w_o, hgrn_lb_logits, hgrn_norm_g, ffn_w_gate, ffn_w_up, ffn_w_dw, ffn_b_dw, ffn_w_down):
    n_hgrn = hgrn_w_q.shape[0]
    w4 = jnp.stack([w.astype(BF16).reshape(n_hgrn, D_MODEL, HGRN_HEADS, HGRN_HEAD_DIM)
                    for w in (hgrn_w_q, hgrn_w_f, hgrn_w_i, hgrn_w_g)], axis=3)
    p = {
        'norm_mix': norm_mix, 'norm_ffn': norm_ffn, 'norm_final': norm_final,
        'conf_w_pw1': conf_w_pw1.astype(BF16), 'conf_b_pw1': conf_b_pw1, 'conf_w_dw': conf_w_dw,
        'conf_b_dw': conf_b_dw, 'conf_ln_g': conf_ln_g, 'conf_ln_b': conf_ln_b,
        'conf_w_pw2': conf_w_pw2.astype(BF16), 'conf_b_pw2': conf_b_pw2,
        'pool_w': pool_w.astype(BF16), 'pool_scale': pool_scale,
        'sconv_w_in': sconv_w_in.astype(BF16), 'sconv_w_dw': sconv_w_dw,
        'sconv_w_out': sconv_w_out.astype(BF16),
        'hgrn_w4': w4.reshape(n_hgrn, D_MODEL, HGRN_HEADS * HGRN_PROJ),
        'hgrn_w_o': hgrn_w_o.astype(BF16), 'hgrn_lb_logits': hgrn_lb_logits,
        'hgrn_norm_g': hgrn_norm_g,
        'ffn_w_gate': ffn_w_gate.astype(BF16), 'ffn_w_up': ffn_w_up.astype(BF16),
        'ffn_w_dw': ffn_w_dw, 'ffn_b_dw': ffn_b_dw, 'ffn_w_down': ffn_w_down.astype(BF16),
    }
    b = x_prompt.shape[0]
    zeros_like_state = lambda s: jnp.zeros((s.shape[0], b) + s.shape[2:], s.dtype)
    y_p, conf_p, pool_p, sconv_p, hgrn_p, ffn_p = _trunk(
        x_prompt, 0, zeros_like_state(state_conformer_conv), zeros_like_state(state_pool),
        zeros_like_state(state_short_conv), zeros_like_state(state_hgrn),
        zeros_like_state(state_ffn_conv), p)
    y_s, conf_s, pool_s, sconv_s, hgrn_s, ffn_s = _trunk(
        x_sample, PAST_LEN, state_conformer_conv, state_pool, state_short_conv, state_hgrn,
        state_ffn_conv, p)
    return (y_p, y_s, conf_p, conf_s, pool_p, pool_s, sconv_p, sconv_s,
            hgrn_p, hgrn_s, ffn_p, ffn_s)
```

```python
import functools
from typing import NamedTuple

import jax
import jax.numpy as jnp
from jax import lax
from jax.experimental import pallas as pl
from jax.experimental.pallas import tpu as pltpu

D_MODEL = 1024
D_FF = 2816
CONF_WIDTH = 31
POOL_WINDOWS = (2, 4, 8, 16)
POOL_GROUP_DIM = D_MODEL // len(POOL_WINDOWS)
POOL_HIST = max(POOL_WINDOWS) - 1
SCONV_WIDTH = 3
FFN_WIDTH = 3
HGRN_HEAD_DIM = 128
HGRN_HEADS = D_MODEL // HGRN_HEAD_DIM
HGRN_CHUNK = 64
PAST_LEN = 2048
EPS = 1e-6

SUBLANES = 8
LANES = 128
TILE_ROWS = 512
VMEM_LIMIT_BYTES = 56 * 1024 * 1024

F32 = jnp.float32
BF16 = jnp.bfloat16


def _round_up(n, m):
    return (n + m - 1) // m * m


def _rms(x, g):
    return x * lax.rsqrt(jnp.mean(x * x, axis=-1, keepdims=True) + EPS) * g


def _dot(a, b):
    return jnp.dot(a, b, preferred_element_type=F32)


def _dot_nt(a, b):
    return lax.dot_general(a, b, (((1,), (1,)), ((), ())), preferred_element_type=F32)


def _dot_tn(a, b):
    return lax.dot_general(a, b, (((0,), (0,)), ((), ())), preferred_element_type=F32)


def _col_chunks(total, width):
    return [(lo, min(lo + width, total)) for lo in range(0, total, width)]


FFN_PAD = SUBLANES
FFN_TILE_ROWS = 512
FFN_CHUNK = 512


def _ffn_kernel(x_ref, hist_ref, ng_ref, wg_ref, wu_ref, wdw_ref, bdw_ref, wd_ref, fin_ref,
                o_ref, nh_ref, cbuf, carry, *, nb, tt, chunks, final_norm):
    m = nb * tt
    h0 = FFN_PAD - (FFN_WIDTH - 1)

    @pl.when(pl.program_id(1) == 0)
    def _():
        carry[:, h0:FFN_PAD, :] = hist_ref[...]

    x = x_ref[...].reshape(m, D_MODEL)
    xn = _rms(x, ng_ref[...]).astype(BF16)
    acc = x

    def up_projections(lo, hi):
        return _dot(xn, wg_ref[:, lo:hi]), _dot(xn, wu_ref[:, lo:hi])

    projected = [up_projections(*c) for c in chunks]
    for ci, (lo, hi) in enumerate(chunks):
        nc = hi - lo
        g, u = projected[ci]
        g = g.reshape(nb, tt, nc)
        cb = cbuf.at[ci % 2]
        cb[:, h0:FFN_PAD, 0:nc] = carry[:, h0:FFN_PAD, lo:hi]
        cb[:, FFN_PAD:FFN_PAD + tt, 0:nc] = g
        g1 = cb[:, FFN_PAD - 1:FFN_PAD - 1 + tt, 0:nc]
        g2 = cb[:, FFN_PAD - 2:FFN_PAD - 2 + tt, 0:nc]
        w = wdw_ref[:, lo:hi]
        c = g * w[2:3] + g1 * w[1:2] + g2 * w[0:1] + bdw_ref[:, lo:hi]
        carry[:, h0:FFN_PAD, lo:hi] = cb[:, h0 + tt:FFN_PAD + tt, 0:nc]
        h = (c * jax.nn.sigmoid(c)).reshape(m, nc) * u
        acc = acc + _dot(h.astype(BF16), wd_ref[lo:hi, :])
    nh_ref[...] = carry[:, h0:FFN_PAD, :]
    if final_norm:
        acc = _rms(acc, fin_ref[...])
    o_ref[...] = acc.reshape(nb, tt, D_MODEL)


CONF_PAD = _round_up(CONF_WIDTH - 1, SUBLANES)
ROW_STRIDE = 4
ROW_GROUP = ROW_STRIDE * SUBLANES
N_SLABS = D_MODEL // LANES


def _conf_kernel(x_ref, hist_ref, ng_ref, w1_ref, b1_ref, wdw_ref, bdw_ref, lng_ref, lnb_ref,
                 w2_ref, b2_ref, o_ref, nh_ref, ubuf, cbuf, *, nb, tt, chunks):
    m = nb * tt
    hw = CONF_WIDTH - 1
    h0 = CONF_PAD - hw

    @pl.when(pl.program_id(1) == 0)
    def _():
        for j in range(N_SLABS):
            ubuf[j, :, h0:CONF_PAD, :] = hist_ref[:, :, j * LANES:(j + 1) * LANES]

    x = x_ref[...].reshape(m, D_MODEL)
    xn = _rms(x, ng_ref[...]).astype(BF16)
    projected = [(_dot(xn, w1_ref[:, lo:hi]), _dot(xn, w1_ref[:, D_MODEL + lo:D_MODEL + hi]))
                 for lo, hi in chunks]
    for (lo, hi), (a, gt) in zip(chunks, projected):
        a = a + b1_ref[:, lo:hi]
        gt = gt + b1_ref[:, D_MODEL + lo:D_MODEL + hi]
        u = a * jax.nn.sigmoid(gt)
        for l0 in range(lo, hi, LANES):
            ubuf[l0 // LANES, :, CONF_PAD:CONF_PAD + tt, :] = (
                u[:, l0 - lo:l0 - lo + LANES].reshape(nb, tt, LANES))

    def slab(j, carry):
        l0 = pl.multiple_of(j * LANES, LANES)
        w = wdw_ref[:, pl.ds(l0, LANES)]
        wk = [jnp.broadcast_to(w[k:k + 1], (SUBLANES, LANES)) for k in range(CONF_WIDTH)]
        bias = jnp.broadcast_to(bdw_ref[:, pl.ds(l0, LANES)], (SUBLANES, LANES))
        for b in range(nb):
            for r0 in range(0, tt, ROW_GROUP):
                for ph in range(ROW_STRIDE):
                    acc = bias
                    for k in range(CONF_WIDTH):
                        rows = pl.ds(h0 + k + r0 + ph, SUBLANES, stride=ROW_STRIDE)
                        acc = acc + ubuf[j, b, rows, :] * wk[k]
                    cbuf[j, b, pl.ds(r0 + ph, SUBLANES, stride=ROW_STRIDE), :] = acc
        return carry

    lax.fori_loop(0, N_SLABS, slab, 0)

    for j in range(N_SLABS):
        nh_ref[:, :, j * LANES:(j + 1) * LANES] = ubuf[j, :, h0 + tt:CONF_PAD + tt, :]
        ubuf[j, :, h0:CONF_PAD, :] = ubuf[j, :, h0 + tt:CONF_PAD + tt, :]

    c = jnp.concatenate([cbuf[j].reshape(m, LANES) for j in range(N_SLABS)], axis=-1)
    mu = jnp.mean(c, axis=-1, keepdims=True)
    cc = c - mu
    var = jnp.mean(cc * cc, axis=-1, keepdims=True)
    cn = cc * lax.rsqrt(var + EPS) * lng_ref[...] + lnb_ref[...]
    s = cn * jax.nn.sigmoid(cn)
    y = _dot(s.astype(BF16), w2_ref[...]) + b2_ref[...]
    o_ref[...] = (x + y).reshape(nb, tt, D_MODEL)


POOL_PAD = _round_up(POOL_HIST, SUBLANES)


def _pool_kernel(x_ref, hist_ref, ng_ref, wp_ref, sc_ref, o_ref, nh_ref, pbuf, dbuf,
                 *, nb, tt, pos0):
    m = nb * tt
    h0 = POOL_PAD - POOL_HIST
    t = pl.program_id(1)

    @pl.when(t == 0)
    def _():
        for j in range(N_SLABS):
            pbuf[j, :, h0:POOL_PAD, :] = hist_ref[:, :, j * LANES:(j + 1) * LANES]

    x = x_ref[...].reshape(m, D_MODEL)
    xn = _rms(x, ng_ref[...])
    for j in range(N_SLABS):
        pbuf[j, :, POOL_PAD:POOL_PAD + tt, :] = xn[:, j * LANES:(j + 1) * LANES].reshape(nb, tt, LANES)

    tile_rows = ROW_STRIDE * lax.broadcasted_iota(jnp.int32, (SUBLANES, 1), 0)
    slabs_per_group = POOL_GROUP_DIM // LANES
    for gi, win in enumerate(POOL_WINDOWS):
        for r0 in range(0, tt, ROW_GROUP):
            for ph in range(ROW_STRIDE):
                pos = pos0 + t * tt + r0 + ph + tile_rows
                inv = 1.0 / jnp.minimum(pos + 1, win).astype(F32)
                for j in range(gi * slabs_per_group, (gi + 1) * slabs_per_group):
                    for b in range(nb):
                        cur = pbuf[j, b, pl.ds(POOL_PAD + r0 + ph, SUBLANES, stride=ROW_STRIDE), :]
                        acc = cur
                        for i in range(1, win):
                            rows = pl.ds(POOL_PAD + r0 + ph - i, SUBLANES, stride=ROW_STRIDE)
                            acc = acc + pbuf[j, b, rows, :]
                        dbuf[j, b, pl.ds(r0 + ph, SUBLANES, stride=ROW_STRIDE), :] = acc * inv - cur

    for j in range(N_SLABS):
        nh_ref[:, :, j * LANES:(j + 1) * LANES] = pbuf[j, :, h0 + tt:POOL_PAD + tt, :]
        pbuf[j, :, h0:POOL_PAD, :] = pbuf[j, :, h0 + tt:POOL_PAD + tt, :]

    for gi in range(len(POOL_WINDOWS)):
        lo, hi = gi * POOL_GROUP_DIM, (gi + 1) * POOL_GROUP_DIM
        d = jnp.concatenate([dbuf[j].reshape(m, LANES) for j in range(lo // LANES, hi // LANES)],
                            axis=-1).astype(BF16)
        y = _dot(d, wp_ref[gi]) * sc_ref[:, lo:hi]
        o_ref[:, :, lo:hi] = (x[:, lo:hi] + y).reshape(nb, tt, POOL_GROUP_DIM)


SCONV_PAD = SUBLANES


def _sconv_kernel(x_ref, hist_ref, ng_ref, win_ref, wdw_ref, wout_ref, o_ref, nh_ref, pbuf,
                  *, nb, tt, chunks):
    m = nb * tt
    h0 = SCONV_PAD - (SCONV_WIDTH - 1)

    @pl.when(pl.program_id(1) == 0)
    def _():
        pbuf[:, h0:SCONV_PAD, :] = hist_ref[...]

    x = x_ref[...].reshape(m, D_MODEL)
    xn = _rms(x, ng_ref[...]).astype(BF16)
    acc = x
    projected = [tuple(_dot(xn, win_ref[:, part * D_MODEL + lo:part * D_MODEL + hi])
                       for part in range(3)) for lo, hi in chunks]
    for (lo, hi), (bg, cg, v) in zip(chunks, projected):
        nc = hi - lo
        p = (cg * v).reshape(nb, tt, nc)
        pbuf[:, SCONV_PAD:SCONV_PAD + tt, lo:hi] = p
        p1 = pbuf[:, SCONV_PAD - 1:SCONV_PAD - 1 + tt, lo:hi]
        p2 = pbuf[:, SCONV_PAD - 2:SCONV_PAD - 2 + tt, lo:hi]
        w = wdw_ref[:, lo:hi]
        c = p * w[2:3] + p1 * w[1:2] + p2 * w[0:1]
        pbuf[:, h0:SCONV_PAD, lo:hi] = pbuf[:, h0 + tt:SCONV_PAD + tt, lo:hi]
        acc = acc + _dot((bg * c.reshape(m, nc)).astype(BF16), wout_ref[lo:hi, :])
    nh_ref[...] = pbuf[:, h0:SCONV_PAD, :]
    o_ref[...] = acc.reshape(nb, tt, D_MODEL)


HGRN_PROJ = 4 * HGRN_HEAD_DIM
HGRN_BLOCK = 256
HGRN_HEAD_GROUP = 8


def _hgrn_kernel(x_ref, s0_ref, ng_ref, w4_ref, wo_ref, lbl_ref, hng_ref, o_ref, sout_ref,
                 st_ref, onbuf, *, nb, tt, layer, group):
    m = nb * tt
    hd = HGRN_HEAD_DIM
    ck = HGRN_CHUNK
    t = pl.program_id(1)

    @pl.when(t == 0)
    def _():
        for b in range(nb):
            for h in range(HGRN_HEADS):
                st_ref[b, h] = s0_ref[b, h].T

    x = x_ref[...].reshape(m, D_MODEL)
    xn = _rms(x, ng_ref[...]).astype(BF16)

    lg = lbl_ref[...]
    e = jnp.exp(lg - jnp.max(lg, axis=0, keepdims=True))
    p = e / jnp.sum(e, axis=0, keepdims=True)
    lb_all = p[1:2]
    for i in range(2, layer + 1):
        lb_all = lb_all + p[i:i + 1]
    if layer == 0:
        lb_all = jnp.zeros_like(p[0:1])

    blk = HGRN_BLOCK
    row = lax.broadcasted_iota(jnp.int32, (blk, blk), 0)
    col = lax.broadcasted_iota(jnp.int32, (blk, blk), 1)
    ck_shift = ck.bit_length() - 1
    causal = (row >= col) & ((row >> ck_shift) == (col >> ck_shift))
    tri = causal.astype(BF16)
    hng = hng_ref[...]

    blocks = list(range(0, m, blk))
    chunk_starts = list(range(0, blk, ck))
    for g0 in range(0, HGRN_HEADS, group):
        heads = list(range(g0, g0 + group))
        q, k, v, gate, lf3 = {}, {}, {}, {}, {}
        for h in heads:
            z = _dot(xn, w4_ref[:, h * HGRN_PROJ:(h + 1) * HGRN_PROJ])
            lb = lb_all[:, h * hd:(h + 1) * hd]
            zq = z[:, 0:hd]
            q[h] = zq * jax.nn.sigmoid(zq)
            f = lb + (1.0 - lb) * jax.nn.sigmoid(z[:, hd:2 * hd])
            k[h] = 1.0 - f
            lf = jnp.log(f)
            v[h] = z[:, 2 * hd:3 * hd].astype(BF16)
            zg = z[:, 3 * hd:4 * hd]
            gate[h] = zg * jax.nn.sigmoid(zg)
            hi = lf.astype(BF16)
            r1 = lf - hi.astype(F32)
            mid = r1.astype(BF16)
            lo = (r1 - mid.astype(F32)).astype(BF16)
            lf3[h] = jnp.concatenate([hi, mid, lo], axis=1)
        units = [(h, r0) for h in heads for r0 in blocks]
        bc3 = {u: _dot(tri, lf3[u[0]][u[1]:u[1] + blk]) for u in units}
        qd, kd, kk, lasts = {}, {}, {}, {}
        for h, r0 in units:
            u = (h, r0)
            b3 = bc3[u]
            bc = b3[:, 0:hd] + b3[:, hd:2 * hd] + b3[:, 2 * hd:3 * hd]
            lasts[u] = [bc[c0 + ck - 1:c0 + ck] for c0 in chunk_starts]
            bl = jnp.concatenate([jnp.broadcast_to(l, (ck, hd)) for l in lasts[u]], axis=0)
            q_b = q[h][r0:r0 + blk]
            k_b = k[h][r0:r0 + blk]
            qd[u] = (q_b * jnp.exp(bc)).astype(BF16)
            kd[u] = (k_b * jnp.exp(-bc)).astype(BF16)
            kk[u] = (k_b * jnp.exp(bl - bc)).astype(BF16)
        sc = {u: jnp.where(causal, _dot_nt(qd[u], kd[u]), 0.0).astype(BF16) for u in units}
        o_intra = {u: _dot(sc[u], v[u[0]][u[1]:u[1] + blk]) for u in units}
        upd = {(h, r0, c0): _dot_tn(v[h][r0 + c0:r0 + c0 + ck], kk[(h, r0)][c0:c0 + ck])
               for h, r0 in units for c0 in chunk_starts}
        st_in = {}
        for h in heads:
            cur = {}
            for r0 in blocks:
                for ci, c0 in enumerate(chunk_starts):
                    b = (r0 + c0) // tt
                    if b not in cur:
                        cur[b] = st_ref[b, h]
                    st_in[(h, r0, c0)] = cur[b].astype(BF16)
                    cur[b] = cur[b] * jnp.exp(lasts[(h, r0)][ci]) + upd[(h, r0, c0)]
            for b, s in cur.items():
                st_ref[b, h] = s
        o_inter = {key: _dot_nt(qd[key[:2]][key[2]:key[2] + ck], st_in[key]) for key in st_in}
        for h, r0 in units:
            o = o_intra[(h, r0)] + jnp.concatenate([o_inter[(h, r0, c0)] for c0 in chunk_starts], axis=0)
            on = o * lax.rsqrt(jnp.mean(o * o, axis=-1, keepdims=True) + EPS) * hng
            onbuf[r0:r0 + blk, h * hd:(h + 1) * hd] = (on * gate[h][r0:r0 + blk]).astype(BF16)

    y = _dot(onbuf[...], wo_ref[...])
    o_ref[...] = (x + y).reshape(nb, tt, D_MODEL)

    @pl.when(t == pl.num_programs(1) - 1)
    def _():
        for b in range(nb):
            for h in range(HGRN_HEADS):
                sout_ref[b, h] = st_ref[b, h].T


def _tiling(batch, seq, rows):
    tt = min(seq, rows)
    nb = max(1, min(batch, rows // tt))
    assert seq % tt == 0 and batch % nb == 0
    assert tt % HGRN_CHUNK == 0 and tt % ROW_GROUP == 0 and (nb * tt) % HGRN_BLOCK == 0
    return nb, tt


class _Layer(NamedTuple):
    stacked: jax.Array
    index: int


def _operand(c):
    return c.stacked if isinstance(c, _Layer) else c


def _resident(c):
    if isinstance(c, _Layer):
        shape = c.stacked.shape[1:]
        index = (c.index,) + (0,) * len(shape)
        return pl.BlockSpec((None,) + shape, lambda b, t: index, pipeline_mode=pl.Buffered(1))
    zeros = (0,) * c.ndim
    return pl.BlockSpec(c.shape, lambda b, t: zeros, pipeline_mode=pl.Buffered(1))


def _x_spec(nb, tt):
    return pl.BlockSpec((nb, tt, D_MODEL), lambda b, t: (b, t, 0))


def _state_spec(nb, shape):
    zeros = (0,) * len(shape)
    return pl.BlockSpec((nb,) + shape, lambda b, t: (b,) + zeros)


def _call(body, name, x, state, consts, scratch, extra_out_shape, rows=TILE_ROWS):
    batch, seq, _ = x.shape
    nb, tt = _tiling(batch, seq, rows)
    state_shape = state.shape[1:]
    return pl.pallas_call(
        functools.partial(body, nb=nb, tt=tt),
        name=name,
        grid=(batch // nb, seq // tt),
        in_specs=[_x_spec(nb, tt), _state_spec(nb, state_shape)] + [_resident(c) for c in consts],
        out_specs=[_x_spec(nb, tt), _state_spec(nb, extra_out_shape)],
        out_shape=[jax.ShapeDtypeStruct(x.shape, F32),
                   jax.ShapeDtypeStruct((batch,) + extra_out_shape, F32)],
        scratch_shapes=scratch(nb, tt),
        compiler_params=pltpu.CompilerParams(
            dimension_semantics=("arbitrary", "arbitrary"),
            vmem_limit_bytes=VMEM_LIMIT_BYTES),
    )(x, state, *[_operand(c) for c in consts])


def _row(v):
    return v.reshape(1, -1)


def _ffn(x, hist, ng, wg, wu, wdw, bdw, wd, fin, final_norm):
    body = functools.partial(_ffn_kernel, chunks=_col_chunks(D_FF, FFN_CHUNK), final_norm=final_norm)
    scratch = lambda nb, tt: [pltpu.VMEM((2, nb, FFN_PAD + tt, FFN_CHUNK), F32),
                              pltpu.VMEM((nb, FFN_PAD, D_FF), F32)]
    return _call(body, "conv_ffn", x, hist,
                 [_row(ng), wg, wu, wdw, _row(bdw), wd, _row(fin)], scratch, (FFN_WIDTH - 1, D_FF),
                 rows=FFN_TILE_ROWS)


def _conf(x, hist, ng, w1, b1, wdw, bdw, lng, lnb, w2, b2):
    body = functools.partial(_conf_kernel, chunks=_col_chunks(D_MODEL, 512))
    scratch = lambda nb, tt: [pltpu.VMEM((N_SLABS, nb, CONF_PAD + tt, LANES), F32),
                              pltpu.VMEM((N_SLABS, nb, tt, LANES), F32)]
    return _call(body, "conformer_conv", x, hist,
                 [_row(ng), w1, _row(b1), wdw, _row(bdw), _row(lng), _row(lnb), w2, _row(b2)],
                 scratch, (CONF_WIDTH - 1, D_MODEL))


def _pool(x, hist, ng, wp, scale, pos0):
    body = functools.partial(_pool_kernel, pos0=pos0)
    scratch = lambda nb, tt: [pltpu.VMEM((N_SLABS, nb, POOL_PAD + tt, LANES), F32),
                              pltpu.VMEM((N_SLABS, nb, tt, LANES), F32)]
    return _call(body, "pool_mixer", x, hist, [_row(ng), wp, _row(scale)], scratch,
                 (POOL_HIST, D_MODEL))


def _sconv(x, hist, ng, win, wdw, wout):
    body = functools.partial(_sconv_kernel, chunks=_col_chunks(D_MODEL, 512))
    scratch = lambda nb, tt: [pltpu.VMEM((nb, SCONV_PAD + tt, D_MODEL), F32)]
    return _call(body, "short_conv", x, hist, [_row(ng), win, wdw, wout], scratch,
                 (SCONV_WIDTH - 1, D_MODEL))


def _hgrn(x, s0, ng, w4, wo, lb_logits, hng, layer):
    body = functools.partial(_hgrn_kernel, layer=layer, group=HGRN_HEAD_GROUP)
    scratch = lambda nb, tt: [pltpu.VMEM((nb, HGRN_HEADS, HGRN_HEAD_DIM, HGRN_HEAD_DIM), F32),
                              pltpu.VMEM((nb * tt, D_MODEL), BF16)]
    return _call(body, "hgrn2", x, s0, [_row(ng), w4, wo, lb_logits, _row(hng)], scratch,
                 (HGRN_HEADS, HGRN_HEAD_DIM, HGRN_HEAD_DIM))


def _trunk(x, pos0, conf_hist, pool_hist, sconv_hist, hgrn_state, ffn_hist, p):
    depth = p['norm_mix'].shape[0]
    new = {'conf': [], 'pool': [], 'sconv': [], 'hgrn': [], 'ffn': []}
    for i in range(depth):
        mixer, j = i % 4, i // 4
        ng = p['norm_mix'][i]
        if mixer == 0:
            x, h = _conf(x, conf_hist[j], ng, p['conf_w_pw1'][j], p['conf_b_pw1'][j],
                         p['conf_w_dw'][j], p['conf_b_dw'][j], p['conf_ln_g'][j], p['conf_ln_b'][j],
                         p['conf_w_pw2'][j], p['conf_b_pw2'][j])
            new['conf'].append(h)
        elif mixer == 1:
            x, h = _pool(x, pool_hist[j], ng, p['pool_w'][j], p['pool_scale'][j], pos0)
            new['pool'].append(h)
        elif mixer == 2:
            x, h = _sconv(x, sconv_hist[j], ng, p['sconv_w_in'][j], p['sconv_w_dw'][j],
                          p['sconv_w_out'][j])
            new['sconv'].append(h)
        else:
            x, h = _hgrn(x, hgrn_state[j], ng, p['hgrn_w4'][j], p['hgrn_w_o'][j],
                         p['hgrn_lb_logits'], p['hgrn_norm_g'][j], i)
            new['hgrn'].append(h)
        x, h = _ffn(x, ffn_hist[i], p['norm_ffn'][i], _Layer(p['ffn_w_gate'], i),
                    _Layer(p['ffn_w_up'], i), p['ffn_w_dw'][i], p['ffn_b_dw'][i],
                    _Layer(p['ffn_w_down'], i), p['norm_final'],
                    final_norm=(i == depth - 1))
        new['ffn'].append(h)
    return (x,) + tuple(jnp.stack(new[k]) for k in ('conf', 'pool', 'sconv', 'hgrn', 'ffn'))


def kernel(x_prompt, x_sample, state_conformer_conv, state_pool, state_short_conv, state_hgrn, state_ffn_conv, norm_mix, norm_ffn, norm_final, conf_w_pw1, conf_b_pw1, conf_w_dw, conf_b_dw, conf_ln_g, conf_ln_b, conf_w_pw2, conf_b_pw2, pool_w, pool_scale, sconv_w_in, sconv_w_dw, sconv_w_out, hgrn_w_q, hgrn_w_f, hgrn_w_i, hgrn_w_g, hgrn_w_o, hgrn_lb_logits, hgrn_norm_g, ffn_w_gate, ffn_w_up, ffn_w_dw, ffn_b_dw, ffn_w_down):
    n_hgrn = hgrn_w_q.shape[0]
    w4 = jnp.stack([w.astype(BF16).reshape(n_hgrn, D_MODEL, HGRN_HEADS, HGRN_HEAD_DIM)
                    for w in (hgrn_w_q, hgrn_w_f, hgrn_w_i, hgrn_w_g)], axis=3)
    p = {
        'norm_mix': norm_mix, 'norm_ffn': norm_ffn, 'norm_final': norm_final,
        'conf_w_pw1': conf_w_pw1.astype(BF16), 'conf_b_pw1': conf_b_pw1, 'conf_w_dw': conf_w_dw,
        'conf_b_dw': conf_b_dw, 'conf_ln_g': conf_ln_g, 'conf_ln_b': conf_ln_b,
        'conf_w_pw2': conf_w_pw2.astype(BF16), 'conf_b_pw2': conf_b_pw2,
        'pool_w': pool_w.astype(BF16), 'pool_scale': pool_scale,
        'sconv_w_in': sconv_w_in.astype(BF16), 'sconv_w_dw': sconv_w_dw,
        'sconv_w_out': sconv_w_out.astype(BF16),
        'hgrn_w4': w4.reshape(n_hgrn, D_MODEL, HGRN_HEADS * HGRN_PROJ),
        'hgrn_w_o': hgrn_w_o.astype(BF16), 'hgrn_lb_logits': hgrn_lb_logits,
        'hgrn_norm_g': hgrn_norm_g,
        'ffn_w_gate': ffn_w_gate.astype(BF16), 'ffn_w_up': ffn_w_up.astype(BF16),
        'ffn_w_dw': ffn_w_dw, 'ffn_b_dw': ffn_b_dw, 'ffn_w_down': ffn_w_down.astype(BF16),
    }
    b = x_prompt.shape[0]
    zeros_like_state = lambda s: jnp.zeros((s.shape[0], b) + s.shape[2:], s.dtype)
    y_p, conf_p, pool_p, sconv_p, hgrn_p, ffn_p = _trunk(
        x_prompt, 0, zeros_like_state(state_conformer_conv), zeros_like_state(state_pool),
        zeros_like_state(state_short_conv), zeros_like_state(state_hgrn),
        zeros_like_state(state_ffn_conv), p)
    y_s, conf_s, pool_s, sconv_s, hgrn_s, ffn_s = _trunk(
        x_sample, PAST_LEN, state_conformer_conv, state_pool, state_short_conv, state_hgrn,
        state_ffn_conv, p)
    return (y_p, y_s, conf_p, conf_s, pool_p, pool_s, sconv_p, sconv_s,
            hgrn_p, hgrn_s, ffn_p, ffn_s)
```

```python
import functools
from typing import NamedTuple

import jax
import jax.numpy as jnp
from jax import lax
from jax.experimental import pallas as pl
from jax.experimental.pallas import tpu as pltpu

D_MODEL = 1024
D_FF = 2816
CONF_WIDTH = 31
POOL_WINDOWS = (2, 4, 8, 16)
POOL_GROUP_DIM = D_MODEL // len(POOL_WINDOWS)
POOL_HIST = max(POOL_WINDOWS) - 1
SCONV_WIDTH = 3
FFN_WIDTH = 3
HGRN_HEAD_DIM = 128
HGRN_HEADS = D_MODEL // HGRN_HEAD_DIM
HGRN_CHUNK = 64
PAST_LEN = 2048
EPS = 1e-6

SUBLANES = 8
LANES = 128
TILE_ROWS = 512
VMEM_LIMIT_BYTES = 56 * 1024 * 1024

F32 = jnp.float32
BF16 = jnp.bfloat16


def _round_up(n, m):
    return (n + m - 1) // m * m


def _rms(x, g):
    return x * lax.rsqrt(jnp.mean(x * x, axis=-1, keepdims=True) + EPS) * g


def _dot(a, b):
    return jnp.dot(a, b, preferred_element_type=F32)


def _dot_nt(a, b):
    return lax.dot_general(a, b, (((1,), (1,)), ((), ())), preferred_element_type=F32)


def _dot_tn(a, b):
    return lax.dot_general(a, b, (((0,), (0,)), ((), ())), preferred_element_type=F32)


def _col_chunks(total, width):
    return [(lo, min(lo + width, total)) for lo in range(0, total, width)]


FFN_PAD = SUBLANES
FFN_TILE_ROWS = 512
FFN_CHUNK = 768


def _ffn_kernel(x_ref, hist_ref, ng_ref, wg_ref, wu_ref, wdw_ref, bdw_ref, wd_ref, fin_ref,
                o_ref, nh_ref, cbuf, carry, *, nb, tt, chunks, final_norm):
    m = nb * tt
    h0 = FFN_PAD - (FFN_WIDTH - 1)

    @pl.when(pl.program_id(1) == 0)
    def _():
        carry[:, h0:FFN_PAD, :] = hist_ref[...]

    x = x_ref[...].reshape(m, D_MODEL)
    xn = _rms(x, ng_ref[...]).astype(BF16)
    acc = x

    def up_projections(lo, hi):
        return _dot(xn, wg_ref[:, lo:hi]), _dot(xn, wu_ref[:, lo:hi])

    projected = [up_projections(*c) for c in chunks]
    for ci, (lo, hi) in enumerate(chunks):
        nc = hi - lo
        g, u = projected[ci]
        g = g.reshape(nb, tt, nc)
        cb = cbuf.at[ci % 2]
        cb[:, h0:FFN_PAD, 0:nc] = carry[:, h0:FFN_PAD, lo:hi]
        cb[:, FFN_PAD:FFN_PAD + tt, 0:nc] = g
        g1 = cb[:, FFN_PAD - 1:FFN_PAD - 1 + tt, 0:nc]
        g2 = cb[:, FFN_PAD - 2:FFN_PAD - 2 + tt, 0:nc]
        w = wdw_ref[:, lo:hi]
        c = g * w[2:3] + g1 * w[1:2] + g2 * w[0:1] + bdw_ref[:, lo:hi]
        carry[:, h0:FFN_PAD, lo:hi] = cb[:, h0 + tt:FFN_PAD + tt, 0:nc]
        h = (c * jax.nn.sigmoid(c)).reshape(m, nc) * u
        acc = acc + _dot(h.astype(BF16), wd_ref[lo:hi, :])
    nh_ref[...] = carry[:, h0:FFN_PAD, :]
    if final_norm:
        acc = _rms(acc, fin_ref[...])
    o_ref[...] = acc.reshape(nb, tt, D_MODEL)


CONF_PAD = _round_up(CONF_WIDTH - 1, SUBLANES)
ROW_STRIDE = 4
ROW_GROUP = ROW_STRIDE * SUBLANES
N_SLABS = D_MODEL // LANES


def _conf_kernel(x_ref, hist_ref, ng_ref, w1_ref, b1_ref, wdw_ref, bdw_ref, lng_ref, lnb_ref,
                 w2_ref, b2_ref, o_ref, nh_ref, ubuf, cbuf, *, nb, tt, chunks):
    m = nb * tt
    hw = CONF_WIDTH - 1
    h0 = CONF_PAD - hw

    @pl.when(pl.program_id(1) == 0)
    def _():
        for j in range(N_SLABS):
            ubuf[j, :, h0:CONF_PAD, :] = hist_ref[:, :, j * LANES:(j + 1) * LANES]

    x = x_ref[...].reshape(m, D_MODEL)
    xn = _rms(x, ng_ref[...]).astype(BF16)
    projected = [(_dot(xn, w1_ref[:, lo:hi]), _dot(xn, w1_ref[:, D_MODEL + lo:D_MODEL + hi]))
                 for lo, hi in chunks]
    for (lo, hi), (a, gt) in zip(chunks, projected):
        a = a + b1_ref[:, lo:hi]
        gt = gt + b1_ref[:, D_MODEL + lo:D_MODEL + hi]
        u = a * jax.nn.sigmoid(gt)
        for l0 in range(lo, hi, LANES):
            ubuf[l0 // LANES, :, CONF_PAD:CONF_PAD + tt, :] = (
                u[:, l0 - lo:l0 - lo + LANES].reshape(nb, tt, LANES))

    def slab(j, carry):
        l0 = pl.multiple_of(j * LANES, LANES)
        w = wdw_ref[:, pl.ds(l0, LANES)]
        wk = [jnp.broadcast_to(w[k:k + 1], (SUBLANES, LANES)) for k in range(CONF_WIDTH)]
        bias = jnp.broadcast_to(bdw_ref[:, pl.ds(l0, LANES)], (SUBLANES, LANES))
        for b in range(nb):
            for r0 in range(0, tt, ROW_GROUP):
                for ph in range(ROW_STRIDE):
                    acc = bias
                    for k in range(CONF_WIDTH):
                        rows = pl.ds(h0 + k + r0 + ph, SUBLANES, stride=ROW_STRIDE)
                        acc = acc + ubuf[j, b, rows, :] * wk[k]
                    cbuf[j, b, pl.ds(r0 + ph, SUBLANES, stride=ROW_STRIDE), :] = acc
        return carry

    lax.fori_loop(0, N_SLABS, slab, 0)

    for j in range(N_SLABS):
        nh_ref[:, :, j * LANES:(j + 1) * LANES] = ubuf[j, :, h0 + tt:CONF_PAD + tt, :]
        ubuf[j, :, h0:CONF_PAD, :] = ubuf[j, :, h0 + tt:CONF_PAD + tt, :]

    c = jnp.concatenate([cbuf[j].reshape(m, LANES) for j in range(N_SLABS)], axis=-1)
    mu = jnp.mean(c, axis=-1, keepdims=True)
    cc = c - mu
    var = jnp.mean(cc * cc, axis=-1, keepdims=True)
    cn = cc * lax.rsqrt(var + EPS) * lng_ref[...] + lnb_ref[...]
    s = cn * jax.nn.sigmoid(cn)
    y = _dot(s.astype(BF16), w2_ref[...]) + b2_ref[...]
    o_ref[...] = (x + y).reshape(nb, tt, D_MODEL)


POOL_PAD = _round_up(POOL_HIST, SUBLANES)


def _pool_kernel(x_ref, hist_ref, ng_ref, wp_ref, sc_ref, o_ref, nh_ref, pbuf, dbuf,
                 *, nb, tt, pos0):
    m = nb * tt
    h0 = POOL_PAD - POOL_HIST
    t = pl.program_id(1)

    @pl.when(t == 0)
    def _():
        for j in range(N_SLABS):
            pbuf[j, :, h0:POOL_PAD, :] = hist_ref[:, :, j * LANES:(j + 1) * LANES]

    x = x_ref[...].reshape(m, D_MODEL)
    xn = _rms(x, ng_ref[...])
    for j in range(N_SLABS):
        pbuf[j, :, POOL_PAD:POOL_PAD + tt, :] = xn[:, j * LANES:(j + 1) * LANES].reshape(nb, tt, LANES)

    tile_rows = ROW_STRIDE * lax.broadcasted_iota(jnp.int32, (SUBLANES, 1), 0)
    slabs_per_group = POOL_GROUP_DIM // LANES
    for gi, win in enumerate(POOL_WINDOWS):
        for r0 in range(0, tt, ROW_GROUP):
            for ph in range(ROW_STRIDE):
                pos = pos0 + t * tt + r0 + ph + tile_rows
                inv = 1.0 / jnp.minimum(pos + 1, win).astype(F32)
                for j in range(gi * slabs_per_group, (gi + 1) * slabs_per_group):
                    for b in range(nb):
                        cur = pbuf[j, b, pl.ds(POOL_PAD + r0 + ph, SUBLANES, stride=ROW_STRIDE), :]
                        acc = cur
                        for i in range(1, win):
                            rows = pl.ds(POOL_PAD + r0 + ph - i, SUBLANES, stride=ROW_STRIDE)
                            acc = acc + pbuf[j, b, rows, :]
                        dbuf[j, b, pl.ds(r0 + ph, SUBLANES, stride=ROW_STRIDE), :] = acc * inv - cur

    for j in range(N_SLABS):
        nh_ref[:, :, j * LANES:(j + 1) * LANES] = pbuf[j, :, h0 + tt:POOL_PAD + tt, :]
        pbuf[j, :, h0:POOL_PAD, :] = pbuf[j, :, h0 + tt:POOL_PAD + tt, :]

    for gi in range(len(POOL_WINDOWS)):
        lo, hi = gi * POOL_GROUP_DIM, (gi + 1) * POOL_GROUP_DIM
        d = jnp.concatenate([dbuf[j].reshape(m, LANES) for j in range(lo // LANES, hi // LANES)],
                            axis=-1).astype(BF16)
        y = _dot(d, wp_ref[gi]) * sc_ref[:, lo:hi]
        o_ref[:, :, lo:hi] = (x[:, lo:hi] + y).reshape(nb, tt, POOL_GROUP_DIM)


SCONV_PAD = SUBLANES


def _sconv_kernel(x_ref, hist_ref, ng_ref, win_ref, wdw_ref, wout_ref, o_ref, nh_ref, pbuf,
                  *, nb, tt, chunks):
    m = nb * tt
    h0 = SCONV_PAD - (SCONV_WIDTH - 1)

    @pl.when(pl.program_id(1) == 0)
    def _():
        pbuf[:, h0:SCONV_PAD, :] = hist_ref[...]

    x = x_ref[...].reshape(m, D_MODEL)
    xn = _rms(x, ng_ref[...]).astype(BF16)
    acc = x
    projected = [tuple(_dot(xn, win_ref[:, part * D_MODEL + lo:part * D_MODEL + hi])
                       for part in range(3)) for lo, hi in chunks]
    for (lo, hi), (bg, cg, v) in zip(chunks, projected):
        nc = hi - lo
        p = (cg * v).reshape(nb, tt, nc)
        pbuf[:, SCONV_PAD:SCONV_PAD + tt, lo:hi] = p
        p1 = pbuf[:, SCONV_PAD - 1:SCONV_PAD - 1 + tt, lo:hi]
        p2 = pbuf[:, SCONV_PAD - 2:SCONV_PAD - 2 + tt, lo:hi]
        w = wdw_ref[:, lo:hi]
        c = p * w[2:3] + p1 * w[1:2] + p2 * w[0:1]
        pbuf[:, h0:SCONV_PAD, lo:hi] = pbuf[:, h0 + tt:SCONV_PAD + tt, lo:hi]
        acc = acc + _dot((bg * c.reshape(m, nc)).astype(BF16), wout_ref[lo:hi, :])
    nh_ref[...] = pbuf[:, h0:SCONV_PAD, :]
    o_ref[...] = acc.reshape(nb, tt, D_MODEL)


HGRN_PROJ = 4 * HGRN_HEAD_DIM
HGRN_BLOCK = 64
HGRN_HEAD_GROUP = 8


def _hgrn_kernel(x_ref, s0_ref, ng_ref, w4_ref, wo_ref, lbl_ref, hng_ref, o_ref, sout_ref,
                 st_ref, onbuf, *, nb, tt, layer, group):
    m = nb * tt
    hd = HGRN_HEAD_DIM
    ck = HGRN_CHUNK
    t = pl.program_id(1)

    @pl.when(t == 0)
    def _():
        for b in range(nb):
            for h in range(HGRN_HEADS):
                st_ref[b, h] = s0_ref[b, h].T

    x = x_ref[...].reshape(m, D_MODEL)
    xn = _rms(x, ng_ref[...]).astype(BF16)

    lg = lbl_ref[...]
    e = jnp.exp(lg - jnp.max(lg, axis=0, keepdims=True))
    p = e / jnp.sum(e, axis=0, keepdims=True)
    lb_all = p[1:2]
    for i in range(2, layer + 1):
        lb_all = lb_all + p[i:i + 1]
    if layer == 0:
        lb_all = jnp.zeros_like(p[0:1])

    blk = HGRN_BLOCK
    row = lax.broadcasted_iota(jnp.int32, (blk, blk), 0)
    col = lax.broadcasted_iota(jnp.int32, (blk, blk), 1)
    ck_shift = ck.bit_length() - 1
    causal = (row >= col) & ((row >> ck_shift) == (col >> ck_shift))
    tri = causal.astype(BF16)
    hng = hng_ref[...]

    blocks = list(range(0, m, blk))
    chunk_starts = list(range(0, blk, ck))
    for g0 in range(0, HGRN_HEADS, group):
        heads = list(range(g0, g0 + group))
        q, k, v, gate, lf3 = {}, {}, {}, {}, {}
        for h in heads:
            z = _dot(xn, w4_ref[:, h * HGRN_PROJ:(h + 1) * HGRN_PROJ])
            lb = lb_all[:, h * hd:(h + 1) * hd]
            zq = z[:, 0:hd]
            q[h] = zq * jax.nn.sigmoid(zq)
            f = lb + (1.0 - lb) * jax.nn.sigmoid(z[:, hd:2 * hd])
            k[h] = 1.0 - f
            lf = jnp.log(f)
            v[h] = z[:, 2 * hd:3 * hd].astype(BF16)
            zg = z[:, 3 * hd:4 * hd]
            gate[h] = zg * jax.nn.sigmoid(zg)
            hi = lf.astype(BF16)
            r1 = lf - hi.astype(F32)
            mid = r1.astype(BF16)
            lo = (r1 - mid.astype(F32)).astype(BF16)
            lf3[h] = jnp.concatenate([hi, mid, lo], axis=1)
        units = [(h, r0) for h in heads for r0 in blocks]
        bc3 = {u: _dot(tri, lf3[u[0]][u[1]:u[1] + blk]) for u in units}
        qd, kd, kk, lasts = {}, {}, {}, {}
        for h, r0 in units:
            u = (h, r0)
            b3 = bc3[u]
            bc = b3[:, 0:hd] + b3[:, hd:2 * hd] + b3[:, 2 * hd:3 * hd]
            lasts[u] = [bc[c0 + ck - 1:c0 + ck] for c0 in chunk_starts]
            bl = jnp.concatenate([jnp.broadcast_to(l, (ck, hd)) for l in lasts[u]], axis=0)
            q_b = q[h][r0:r0 + blk]
            k_b = k[h][r0:r0 + blk]
            qd[u] = (q_b * jnp.exp(bc)).astype(BF16)
            kd[u] = (k_b * jnp.exp(-bc)).astype(BF16)
            kk[u] = (k_b * jnp.exp(bl - bc)).astype(BF16)
        sc = {u: jnp.where(causal, _dot_nt(qd[u], kd[u]), 0.0).astype(BF16) for u in units}
        o_intra = {u: _dot(sc[u], v[u[0]][u[1]:u[1] + blk]) for u in units}
        upd = {(h, r0, c0): _dot_tn(v[h][r0 + c0:r0 + c0 + ck], kk[(h, r0)][c0:c0 + ck])
               for h, r0 in units for c0 in chunk_starts}
        st_in = {}
        for h in heads:
            cur = {}
            for r0 in blocks:
                for ci, c0 in enumerate(chunk_starts):
                    b = (r0 + c0) // tt
                    if b not in cur:
                        cur[b] = st_ref[b, h]
                    st_in[(h, r0, c0)] = cur[b].astype(BF16)
                    cur[b] = cur[b] * jnp.exp(lasts[(h, r0)][ci]) + upd[(h, r0, c0)]
            for b, s in cur.items():
                st_ref[b, h] = s
        o_inter = {key: _dot_nt(qd[key[:2]][key[2]:key[2] + ck], st_in[key]) for key in st_in}
        for h, r0 in units:
            o = o_intra[(h, r0)] + jnp.concatenate([o_inter[(h, r0, c0)] for c0 in chunk_starts], axis=0)
            on = o * lax.rsqrt(jnp.mean(o * o, axis=-1, keepdims=True) + EPS) * hng
            onbuf[r0:r0 + blk, h * hd:(h + 1) * hd] = (on * gate[h][r0:r0 + blk]).astype(BF16)

    y = _dot(onbuf[...], wo_ref[...])
    o_ref[...] = (x + y).reshape(nb, tt, D_MODEL)

    @pl.when(t == pl.num_programs(1) - 1)
    def _():
        for b in range(nb):
            for h in range(HGRN_HEADS):
                sout_ref[b, h] = st_ref[b, h].T


def _tiling(batch, seq, rows):
    tt = min(seq, rows)
    nb = max(1, min(batch, rows // tt))
    assert seq % tt == 0 and batch % nb == 0
    assert tt % HGRN_CHUNK == 0 and tt % ROW_GROUP == 0 and (nb * tt) % HGRN_BLOCK == 0
    return nb, tt


class _Layer(NamedTuple):
    stacked: jax.Array
    index: int


def _operand(c):
    return c.stacked if isinstance(c, _Layer) else c


def _resident(c):
    if isinstance(c, _Layer):
        shape = c.stacked.shape[1:]
        index = (c.index,) + (0,) * len(shape)
        return pl.BlockSpec((None,) + shape, lambda b, t: index, pipeline_mode=pl.Buffered(1))
    zeros = (0,) * c.ndim
    return pl.BlockSpec(c.shape, lambda b, t: zeros, pipeline_mode=pl.Buffered(1))


def _x_spec(nb, tt):
    return pl.BlockSpec((nb, tt, D_MODEL), lambda b, t: (b, t, 0))


def _state_spec(nb, shape):
    zeros = (0,) * len(shape)
    return pl.BlockSpec((nb,) + shape, lambda b, t: (b,) + zeros)


def _call(body, name, x, state, consts, scratch, extra_out_shape, rows=TILE_ROWS):
    batch, seq, _ = x.shape
    nb, tt = _tiling(batch, seq, rows)
    state_shape = state.shape[1:]
    return pl.pallas_call(
        functools.partial(body, nb=nb, tt=tt),
        name=name,
        grid=(batch // nb, seq // tt),
        in_specs=[_x_spec(nb, tt), _state_spec(nb, state_shape)] + [_resident(c) for c in consts],
        out_specs=[_x_spec(nb, tt), _state_spec(nb, extra_out_shape)],
        out_shape=[jax.ShapeDtypeStruct(x.shape, F32),
                   jax.ShapeDtypeStruct((batch,) + extra_out_shape, F32)],
        scratch_shapes=scratch(nb, tt),
        compiler_params=pltpu.CompilerParams(
            dimension_semantics=("arbitrary", "arbitrary"),
            vmem_limit_bytes=VMEM_LIMIT_BYTES),
    )(x, state, *[_operand(c) for c in consts])


def _row(v):
    return v.reshape(1, -1)


def _ffn(x, hist, ng, wg, wu, wdw, bdw, wd, fin, final_norm):
    body = functools.partial(_ffn_kernel, chunks=_col_chunks(D_FF, FFN_CHUNK), final_norm=final_norm)
    scratch = lambda nb, tt: [pltpu.VMEM((2, nb, FFN_PAD + tt, FFN_CHUNK), F32),
                              pltpu.VMEM((nb, FFN_PAD, D_FF), F32)]
    return _call(body, "conv_ffn", x, hist,
                 [_row(ng), wg, wu, wdw, _row(bdw), wd, _row(fin)], scratch, (FFN_WIDTH - 1, D_FF),
                 rows=FFN_TILE_ROWS)


def _conf(x, hist, ng, w1, b1, wdw, bdw, lng, lnb, w2, b2):
    body = functools.partial(_conf_kernel, chunks=_col_chunks(D_MODEL, 512))
    scratch = lambda nb, tt: [pltpu.VMEM((N_SLABS, nb, CONF_PAD + tt, LANES), F32),
                              pltpu.VMEM((N_SLABS, nb, tt, LANES), F32)]
    return _call(body, "conformer_conv", x, hist,
                 [_row(ng), w1, _row(b1), wdw, _row(bdw), _row(lng), _row(lnb), w2, _row(b2)],
                 scratch, (CONF_WIDTH - 1, D_MODEL))


def _pool(x, hist, ng, wp, scale, pos0):
    body = functools.partial(_pool_kernel, pos0=pos0)
    scratch = lambda nb, tt: [pltpu.VMEM((N_SLABS, nb, POOL_PAD + tt, LANES), F32),
                              pltpu.VMEM((N_SLABS, nb, tt, LANES), F32)]
    return _call(body, "pool_mixer", x, hist, [_row(ng), wp, _row(scale)], scratch,
                 (POOL_HIST, D_MODEL))


def _sconv(x, hist, ng, win, wdw, wout):
    body = functools.partial(_sconv_kernel, chunks=_col_chunks(D_MODEL, 256))
    scratch = lambda nb, tt: [pltpu.VMEM((nb, SCONV_PAD + tt, D_MODEL), F32)]
    return _call(body, "short_conv", x, hist, [_row(ng), win, wdw, wout], scratch,
                 (SCONV_WIDTH - 1, D_MODEL))


def _hgrn(x, s0, ng, w4, wo, lb_logits, hng, layer):
    body = functools.partial(_hgrn_kernel, layer=layer, group=HGRN_HEAD_GROUP)
    scratch = lambda nb, tt: [pltpu.VMEM((nb, HGRN_HEADS, HGRN_HEAD_DIM, HGRN_HEAD_DIM), F32),
                              pltpu.VMEM((nb * tt, D_MODEL), BF16)]
    return _call(body, "hgrn2", x, s0, [_row(ng), w4, wo, lb_logits, _row(hng)], scratch,
                 (HGRN_HEADS, HGRN_HEAD_DIM, HGRN_HEAD_DIM))


def _trunk(x, pos0, conf_hist, pool_hist, sconv_hist, hgrn_state, ffn_hist, p):
    depth = p['norm_mix'].shape[0]
    new = {'conf': [], 'pool': [], 'sconv': [], 'hgrn': [], 'ffn': []}
    for i in range(depth):
        mixer, j = i % 4, i // 4
        ng = p['norm_mix'][i]
        if mixer == 0:
            x, h = _conf(x, conf_hist[j], ng, p['conf_w_pw1'][j], p['conf_b_pw1'][j],
                         p['conf_w_dw'][j], p['conf_b_dw'][j], p['conf_ln_g'][j], p['conf_ln_b'][j],
                         p['conf_w_pw2'][j], p['conf_b_pw2'][j])
            new['conf'].append(h)
        elif mixer == 1:
            x, h = _pool(x, pool_hist[j], ng, p['pool_w'][j], p['pool_scale'][j], pos0)
            new['pool'].append(h)
        elif mixer == 2:
            x, h = _sconv(x, sconv_hist[j], ng, p['sconv_w_in'][j], p['sconv_w_dw'][j],
                          p['sconv_w_out'][j])
            new['sconv'].append(h)
        else:
            x, h = _hgrn(x, hgrn_state[j], ng, p['hgrn_w4'][j], p['hgrn_w_o'][j],
                         p['hgrn_lb_logits'], p['hgrn_norm_g'][j], i)
            new['hgrn'].append(h)
        x, h = _ffn(x, ffn_hist[i], p['norm_ffn'][i], _Layer(p['ffn_w_gate'], i),
                    _Layer(p['ffn_w_up'], i), p['ffn_w_dw'][i], p['ffn_b_dw'][i],
                    _Layer(p['ffn_w_down'], i), p['norm_final'],
                    final_norm=(i == depth - 1))
        new['ffn'].append(h)
    return (x,) + tuple(jnp.stack(new[k]) for k in ('conf', 'pool', 'sconv', 'hgrn', 'ffn'))


def kernel(x_prompt, x_sample, state_conformer_conv, state_pool, state_short_conv, state_hgrn, state_ffn_conv, norm_mix, norm_ffn, norm_final, conf_w_pw1, conf_b_pw1, conf_w_dw, conf_b_dw, conf_ln_g, conf_ln_b, conf_w_pw2, conf_b_pw2, pool_w, pool_scale, sconv_w_in, sconv_w_dw, sconv_w_out, hgrn_w_q, hgrn_w_f, hgrn_w_i, hgrn_w_g, hgrn_w_o, hgrn_lb_logits, hgrn_norm_g, ffn_w_gate, ffn_w_up, ffn_w_dw, ffn_b_dw, ffn_w_down):
    w4 = jnp.concatenate(
        [w[:, :, h * HGRN_HEAD_DIM:(h + 1) * HGRN_HEAD_DIM]
         for h in range(HGRN_HEADS) for w in (hgrn_w_q, hgrn_w_f, hgrn_w_i, hgrn_w_g)],
        axis=-1).astype(BF16)
    p = {
        'norm_mix': norm_mix, 'norm_ffn': norm_ffn, 'norm_final': norm_final,
        'conf_w_pw1': conf_w_pw1.astype(BF16), 'conf_b_pw1': conf_b_pw1, 'conf_w_dw': conf_w_dw,
        'conf_b_dw': conf_b_dw, 'conf_ln_g': conf_ln_g, 'conf_ln_b': conf_ln_b,
        'conf_w_pw2': conf_w_pw2.astype(BF16), 'conf_b_pw2': conf_b_pw2,
        'pool_w': pool_w.astype(BF16), 'pool_scale': pool_scale,
        'sconv_w_in': sconv_w_in.astype(BF16), 'sconv_w_dw': sconv_w_dw,
        'sconv_w_out': sconv_w_out.astype(BF16),
        'hgrn_w4': w4,
        'hgrn_w_o': hgrn_w_o.astype(BF16), 'hgrn_lb_logits': hgrn_lb_logits,
        'hgrn_norm_g': hgrn_norm_g,
        'ffn_w_gate': ffn_w_gate.astype(BF16), 'ffn_w_up': ffn_w_up.astype(BF16),
        'ffn_w_dw': ffn_w_dw, 'ffn_b_dw': ffn_b_dw, 'ffn_w_down': ffn_w_down.astype(BF16),
    }
    b = x_prompt.shape[0]
    zeros_like_state = lambda s: jnp.zeros((s.shape[0], b) + s.shape[2:], s.dtype)
    y_p, conf_p, pool_p, sconv_p, hgrn_p, ffn_p = _trunk(
        x_prompt, 0, zeros_like_state(state_conformer_conv), zeros_like_state(state_pool),
        zeros_like_state(state_short_conv), zeros_like_state(state_hgrn),
        zeros_like_state(state_ffn_conv), p)
    y_s, conf_s, pool_s, sconv_s, hgrn_s, ffn_s = _trunk(
        x_sample, PAST_LEN, state_conformer_conv, state_pool, state_short_conv, state_hgrn,
        state_ffn_conv, p)
    return (y_p, y_s, conf_p, conf_s, pool_p, pool_s, sconv_p, sconv_s,
            hgrn_p, hgrn_s, ffn_p, ffn_s)
```

```python
import functools
from typing import NamedTuple

import jax
import jax.numpy as jnp
from jax import lax
from jax.experimental import pallas as pl
from jax.experimental.pallas import tpu as pltpu

D_MODEL = 1024
D_FF = 2816
CONF_WIDTH = 31
POOL_WINDOWS = (2, 4, 8, 16)
POOL_GROUP_DIM = D_MODEL // len(POOL_WINDOWS)
POOL_HIST = max(POOL_WINDOWS) - 1
SCONV_WIDTH = 3
FFN_WIDTH = 3
HGRN_HEAD_DIM = 128
HGRN_HEADS = D_MODEL // HGRN_HEAD_DIM
HGRN_CHUNK = 64
PAST_LEN = 2048
EPS = 1e-6

SUBLANES = 8
LANES = 128
TILE_ROWS = 512
VMEM_LIMIT_BYTES = 56 * 1024 * 1024

F32 = jnp.float32
BF16 = jnp.bfloat16


def _round_up(n, m):
    return (n + m - 1) // m * m


def _rms(x, g):
    return x * lax.rsqrt(jnp.mean(x * x, axis=-1, keepdims=True) + EPS) * g


def _dot(a, b):
    return jnp.dot(a, b, preferred_element_type=F32)


def _dot_nt(a, b):
    return lax.dot_general(a, b, (((1,), (1,)), ((), ())), preferred_element_type=F32)


def _dot_tn(a, b):
    return lax.dot_general(a, b, (((0,), (0,)), ((), ())), preferred_element_type=F32)


def _col_chunks(total, width):
    return [(lo, min(lo + width, total)) for lo in range(0, total, width)]


FFN_PAD = SUBLANES
FFN_CHUNK = 768


def _ffn_kernel(x_ref, hist_ref, ng_ref, wg_ref, wu_ref, wdw_ref, bdw_ref, wd_ref, fin_ref,
                o_ref, nh_ref, cbuf, carry, *, nb, tt, chunks, final_norm):
    m = nb * tt
    h0 = FFN_PAD - (FFN_WIDTH - 1)

    @pl.when(pl.program_id(1) == 0)
    def _():
        carry[:, h0:FFN_PAD, :] = hist_ref[...]

    x = x_ref[...].reshape(m, D_MODEL)
    xn = _rms(x, ng_ref[...]).astype(BF16)
    acc = x

    def up_projections(lo, hi):
        return _dot(xn, wg_ref[:, lo:hi]), _dot(xn, wu_ref[:, lo:hi])

    projected = [up_projections(*c) for c in chunks]
    for ci, (lo, hi) in enumerate(chunks):
        nc = hi - lo
        g, u = projected[ci]
        g = g.reshape(nb, tt, nc)
        cb = cbuf.at[ci % 2]
        cb[:, h0:FFN_PAD, 0:nc] = carry[:, h0:FFN_PAD, lo:hi]
        cb[:, FFN_PAD:FFN_PAD + tt, 0:nc] = g
        g1 = cb[:, FFN_PAD - 1:FFN_PAD - 1 + tt, 0:nc]
        g2 = cb[:, FFN_PAD - 2:FFN_PAD - 2 + tt, 0:nc]
        w = wdw_ref[:, lo:hi]
        c = g * w[2:3] + g1 * w[1:2] + g2 * w[0:1] + bdw_ref[:, lo:hi]
        carry[:, h0:FFN_PAD, lo:hi] = cb[:, h0 + tt:FFN_PAD + tt, 0:nc]
        h = (c * jax.nn.sigmoid(c)).reshape(m, nc) * u
        acc = acc + _dot(h.astype(BF16), wd_ref[lo:hi, :])
    nh_ref[...] = carry[:, h0:FFN_PAD, :]
    if final_norm:
        acc = _rms(acc, fin_ref[...])
    o_ref[...] = acc.reshape(nb, tt, D_MODEL)


CONF_PAD = _round_up(CONF_WIDTH - 1, SUBLANES)
CONF_CHUNK = 512
ROW_STRIDE = 4
ROW_GROUP = ROW_STRIDE * SUBLANES
N_SLABS = D_MODEL // LANES


def _conf_kernel(x_ref, hist_ref, ng_ref, w1_ref, b1_ref, wdw_ref, bdw_ref, lng_ref, lnb_ref,
                 w2_ref, b2_ref, o_ref, nh_ref, ubuf, cbuf, *, nb, tt, chunks):
    m = nb * tt
    hw = CONF_WIDTH - 1
    h0 = CONF_PAD - hw

    @pl.when(pl.program_id(1) == 0)
    def _():
        for j in range(N_SLABS):
            ubuf[j, :, h0:CONF_PAD, :] = hist_ref[:, :, j * LANES:(j + 1) * LANES]

    x = x_ref[...].reshape(m, D_MODEL)
    xn = _rms(x, ng_ref[...]).astype(BF16)
    projected = [(_dot(xn, w1_ref[:, lo:hi]), _dot(xn, w1_ref[:, D_MODEL + lo:D_MODEL + hi]))
                 for lo, hi in chunks]
    for (lo, hi), (a, gt) in zip(chunks, projected):
        a = a + b1_ref[:, lo:hi]
        gt = gt + b1_ref[:, D_MODEL + lo:D_MODEL + hi]
        u = a * jax.nn.sigmoid(gt)
        for l0 in range(lo, hi, LANES):
            ubuf[l0 // LANES, :, CONF_PAD:CONF_PAD + tt, :] = (
                u[:, l0 - lo:l0 - lo + LANES].reshape(nb, tt, LANES))

    def slab(j, carry):
        l0 = pl.multiple_of(j * LANES, LANES)
        w = wdw_ref[:, pl.ds(l0, LANES)]
        wk = [jnp.broadcast_to(w[k:k + 1], (SUBLANES, LANES)) for k in range(CONF_WIDTH)]
        bias = jnp.broadcast_to(bdw_ref[:, pl.ds(l0, LANES)], (SUBLANES, LANES))
        for b in range(nb):
            for r0 in range(0, tt, ROW_GROUP):
                for ph in range(ROW_STRIDE):
                    acc = bias
                    for k in range(CONF_WIDTH):
                        rows = pl.ds(h0 + k + r0 + ph, SUBLANES, stride=ROW_STRIDE)
                        acc = acc + ubuf[j, b, rows, :] * wk[k]
                    cbuf[j, b, pl.ds(r0 + ph, SUBLANES, stride=ROW_STRIDE), :] = acc
        return carry

    lax.fori_loop(0, N_SLABS, slab, 0)

    for j in range(N_SLABS):
        nh_ref[:, :, j * LANES:(j + 1) * LANES] = ubuf[j, :, h0 + tt:CONF_PAD + tt, :]
        ubuf[j, :, h0:CONF_PAD, :] = ubuf[j, :, h0 + tt:CONF_PAD + tt, :]

    c = jnp.concatenate([cbuf[j].reshape(m, LANES) for j in range(N_SLABS)], axis=-1)
    mu = jnp.mean(c, axis=-1, keepdims=True)
    cc = c - mu
    var = jnp.mean(cc * cc, axis=-1, keepdims=True)
    cn = cc * lax.rsqrt(var + EPS) * lng_ref[...] + lnb_ref[...]
    s = cn * jax.nn.sigmoid(cn)
    y = _dot(s.astype(BF16), w2_ref[...]) + b2_ref[...]
    o_ref[...] = (x + y).reshape(nb, tt, D_MODEL)


POOL_PAD = _round_up(POOL_HIST, SUBLANES)


def _pool_kernel(x_ref, hist_ref, ng_ref, wp_ref, sc_ref, o_ref, nh_ref, pbuf, dbuf,
                 *, nb, tt, pos0):
    m = nb * tt
    h0 = POOL_PAD - POOL_HIST
    t = pl.program_id(1)

    @pl.when(t == 0)
    def _():
        for j in range(N_SLABS):
            pbuf[j, :, h0:POOL_PAD, :] = hist_ref[:, :, j * LANES:(j + 1) * LANES]

    x = x_ref[...].reshape(m, D_MODEL)
    xn = _rms(x, ng_ref[...])
    for j in range(N_SLABS):
        pbuf[j, :, POOL_PAD:POOL_PAD + tt, :] = xn[:, j * LANES:(j + 1) * LANES].reshape(nb, tt, LANES)

    tile_rows = ROW_STRIDE * lax.broadcasted_iota(jnp.int32, (SUBLANES, 1), 0)
    slabs_per_group = POOL_GROUP_DIM // LANES
    for gi, win in enumerate(POOL_WINDOWS):
        for r0 in range(0, tt, ROW_GROUP):
            for ph in range(ROW_STRIDE):
                pos = pos0 + t * tt + r0 + ph + tile_rows
                inv = 1.0 / jnp.minimum(pos + 1, win).astype(F32)
                for j in range(gi * slabs_per_group, (gi + 1) * slabs_per_group):
                    for b in range(nb):
                        cur = pbuf[j, b, pl.ds(POOL_PAD + r0 + ph, SUBLANES, stride=ROW_STRIDE), :]
                        acc = cur
                        for i in range(1, win):
                            rows = pl.ds(POOL_PAD + r0 + ph - i, SUBLANES, stride=ROW_STRIDE)
                            acc = acc + pbuf[j, b, rows, :]
                        dbuf[j, b, pl.ds(r0 + ph, SUBLANES, stride=ROW_STRIDE), :] = acc * inv - cur

    for j in range(N_SLABS):
        nh_ref[:, :, j * LANES:(j + 1) * LANES] = pbuf[j, :, h0 + tt:POOL_PAD + tt, :]
        pbuf[j, :, h0:POOL_PAD, :] = pbuf[j, :, h0 + tt:POOL_PAD + tt, :]

    for gi in range(len(POOL_WINDOWS)):
        lo, hi = gi * POOL_GROUP_DIM, (gi + 1) * POOL_GROUP_DIM
        d = jnp.concatenate([dbuf[j].reshape(m, LANES) for j in range(lo // LANES, hi // LANES)],
                            axis=-1).astype(BF16)
        y = _dot(d, wp_ref[gi]) * sc_ref[:, lo:hi]
        o_ref[:, :, lo:hi] = (x[:, lo:hi] + y).reshape(nb, tt, POOL_GROUP_DIM)


SCONV_PAD = SUBLANES
SCONV_CHUNK = 256


def _sconv_kernel(x_ref, hist_ref, ng_ref, win_ref, wdw_ref, wout_ref, o_ref, nh_ref, pbuf,
                  *, nb, tt, chunks):
    m = nb * tt
    h0 = SCONV_PAD - (SCONV_WIDTH - 1)

    @pl.when(pl.program_id(1) == 0)
    def _():
        pbuf[:, h0:SCONV_PAD, :] = hist_ref[...]

    x = x_ref[...].reshape(m, D_MODEL)
    xn = _rms(x, ng_ref[...]).astype(BF16)
    acc = x
    projected = [tuple(_dot(xn, win_ref[:, part * D_MODEL + lo:part * D_MODEL + hi])
                       for part in range(3)) for lo, hi in chunks]
    for (lo, hi), (bg, cg, v) in zip(chunks, projected):
        nc = hi - lo
        p = (cg * v).reshape(nb, tt, nc)
        pbuf[:, SCONV_PAD:SCONV_PAD + tt, lo:hi] = p
        p1 = pbuf[:, SCONV_PAD - 1:SCONV_PAD - 1 + tt, lo:hi]
        p2 = pbuf[:, SCONV_PAD - 2:SCONV_PAD - 2 + tt, lo:hi]
        w = wdw_ref[:, lo:hi]
        c = p * w[2:3] + p1 * w[1:2] + p2 * w[0:1]
        pbuf[:, h0:SCONV_PAD, lo:hi] = pbuf[:, h0 + tt:SCONV_PAD + tt, lo:hi]
        acc = acc + _dot((bg * c.reshape(m, nc)).astype(BF16), wout_ref[lo:hi, :])
    nh_ref[...] = pbuf[:, h0:SCONV_PAD, :]
    o_ref[...] = acc.reshape(nb, tt, D_MODEL)


HGRN_PROJ = 4 * HGRN_HEAD_DIM
HGRN_BLOCK = 64
HGRN_HEAD_GROUP = 8


def _hgrn_kernel(x_ref, s0_ref, ng_ref, w4_ref, wo_ref, lbl_ref, hng_ref, o_ref, sout_ref,
                 st_ref, onbuf, *, nb, tt, layer, group):
    m = nb * tt
    hd = HGRN_HEAD_DIM
    ck = HGRN_CHUNK
    t = pl.program_id(1)

    @pl.when(t == 0)
    def _():
        for b in range(nb):
            for h in range(HGRN_HEADS):
                st_ref[b, h] = s0_ref[b, h].T

    x = x_ref[...].reshape(m, D_MODEL)
    xn = _rms(x, ng_ref[...]).astype(BF16)

    lg = lbl_ref[...]
    e = jnp.exp(lg - jnp.max(lg, axis=0, keepdims=True))
    p = e / jnp.sum(e, axis=0, keepdims=True)
    lb_all = p[1:2]
    for i in range(2, layer + 1):
        lb_all = lb_all + p[i:i + 1]
    if layer == 0:
        lb_all = jnp.zeros_like(p[0:1])

    blk = HGRN_BLOCK
    row = lax.broadcasted_iota(jnp.int32, (blk, blk), 0)
    col = lax.broadcasted_iota(jnp.int32, (blk, blk), 1)
    ck_shift = ck.bit_length() - 1
    causal = (row >= col) & ((row >> ck_shift) == (col >> ck_shift))
    tri = causal.astype(BF16)
    hng = hng_ref[...]

    blocks = list(range(0, m, blk))
    chunk_starts = list(range(0, blk, ck))
    for g0 in range(0, HGRN_HEADS, group):
        heads = list(range(g0, g0 + group))
        q, k, v, gate, lf3 = {}, {}, {}, {}, {}
        for h in heads:
            z = _dot(xn, w4_ref[:, h * HGRN_PROJ:(h + 1) * HGRN_PROJ])
            lb = lb_all[:, h * hd:(h + 1) * hd]
            zq = z[:, 0:hd]
            q[h] = zq * jax.nn.sigmoid(zq)
            f = lb + (1.0 - lb) * jax.nn.sigmoid(z[:, hd:2 * hd])
            k[h] = 1.0 - f
            lf = jnp.log(f)
            v[h] = z[:, 2 * hd:3 * hd].astype(BF16)
            zg = z[:, 3 * hd:4 * hd]
            gate[h] = zg * jax.nn.sigmoid(zg)
            hi = lf.astype(BF16)
            r1 = lf - hi.astype(F32)
            mid = r1.astype(BF16)
            lo = (r1 - mid.astype(F32)).astype(BF16)
            lf3[h] = jnp.concatenate([hi, mid, lo], axis=1)
        units = [(h, r0) for h in heads for r0 in blocks]
        bc3 = {u: _dot(tri, lf3[u[0]][u[1]:u[1] + blk]) for u in units}
        qd, kd, kk, lasts = {}, {}, {}, {}
        for h, r0 in units:
            u = (h, r0)
            b3 = bc3[u]
            bc = b3[:, 0:hd] + b3[:, hd:2 * hd] + b3[:, 2 * hd:3 * hd]
            lasts[u] = [bc[c0 + ck - 1:c0 + ck] for c0 in chunk_starts]
            bl = jnp.concatenate([jnp.broadcast_to(l, (ck, hd)) for l in lasts[u]], axis=0)
            q_b = q[h][r0:r0 + blk]
            k_b = k[h][r0:r0 + blk]
            qd[u] = (q_b * jnp.exp(bc)).astype(BF16)
            kd[u] = (k_b * jnp.exp(-bc)).astype(BF16)
            kk[u] = (k_b * jnp.exp(bl - bc)).astype(BF16)
        sc = {u: jnp.where(causal, _dot_nt(qd[u], kd[u]), 0.0).astype(BF16) for u in units}
        o_intra = {u: _dot(sc[u], v[u[0]][u[1]:u[1] + blk]) for u in units}
        upd = {(h, r0, c0): _dot_tn(v[h][r0 + c0:r0 + c0 + ck], kk[(h, r0)][c0:c0 + ck])
               for h, r0 in units for c0 in chunk_starts}
        st_in = {}
        for h in heads:
            cur = {}
            for r0 in blocks:
                for ci, c0 in enumerate(chunk_starts):
                    b = (r0 + c0) // tt
                    if b not in cur:
                        cur[b] = st_ref[b, h]
                    st_in[(h, r0, c0)] = cur[b].astype(BF16)
                    cur[b] = cur[b] * jnp.exp(lasts[(h, r0)][ci]) + upd[(h, r0, c0)]
            for b, s in cur.items():
                st_ref[b, h] = s
        o_inter = {key: _dot_nt(qd[key[:2]][key[2]:key[2] + ck], st_in[key]) for key in st_in}
        for h, r0 in units:
            o = o_intra[(h, r0)] + jnp.concatenate([o_inter[(h, r0, c0)] for c0 in chunk_starts], axis=0)
            on = o * lax.rsqrt(jnp.mean(o * o, axis=-1, keepdims=True) + EPS) * hng
            onbuf[r0:r0 + blk, h * hd:(h + 1) * hd] = (on * gate[h][r0:r0 + blk]).astype(BF16)

    y = _dot(onbuf[...], wo_ref[...])
    o_ref[...] = (x + y).reshape(nb, tt, D_MODEL)

    @pl.when(t == pl.num_programs(1) - 1)
    def _():
        for b in range(nb):
            for h in range(HGRN_HEADS):
                sout_ref[b, h] = st_ref[b, h].T


def _tiling(batch, seq, rows):
    tt = min(seq, rows)
    nb = max(1, min(batch, rows // tt))
    assert seq % tt == 0 and batch % nb == 0
    assert tt % HGRN_CHUNK == 0 and tt % ROW_GROUP == 0 and (nb * tt) % HGRN_BLOCK == 0
    return nb, tt


class _Layer(NamedTuple):
    stacked: jax.Array
    index: int


def _operand(c):
    return c.stacked if isinstance(c, _Layer) else c


def _resident(c):
    if isinstance(c, _Layer):
        shape = c.stacked.shape[1:]
        index = (c.index,) + (0,) * len(shape)
        return pl.BlockSpec((None,) + shape, lambda b, t: index, pipeline_mode=pl.Buffered(1))
    zeros = (0,) * c.ndim
    return pl.BlockSpec(c.shape, lambda b, t: zeros, pipeline_mode=pl.Buffered(1))


def _x_spec(nb, tt):
    return pl.BlockSpec((nb, tt, D_MODEL), lambda b, t: (b, t, 0))


def _state_spec(nb, shape):
    zeros = (0,) * len(shape)
    return pl.BlockSpec((nb,) + shape, lambda b, t: (b,) + zeros)


def _call(body, name, x, state, consts, scratch, extra_out_shape):
    batch, seq, _ = x.shape
    nb, tt = _tiling(batch, seq, TILE_ROWS)
    state_shape = state.shape[1:]
    return pl.pallas_call(
        functools.partial(body, nb=nb, tt=tt),
        name=name,
        grid=(batch // nb, seq // tt),
        in_specs=[_x_spec(nb, tt), _state_spec(nb, state_shape)] + [_resident(c) for c in consts],
        out_specs=[_x_spec(nb, tt), _state_spec(nb, extra_out_shape)],
        out_shape=[jax.ShapeDtypeStruct(x.shape, F32),
                   jax.ShapeDtypeStruct((batch,) + extra_out_shape, F32)],
        scratch_shapes=scratch(nb, tt),
        compiler_params=pltpu.CompilerParams(
            dimension_semantics=("arbitrary", "arbitrary"),
            vmem_limit_bytes=VMEM_LIMIT_BYTES),
    )(x, state, *[_operand(c) for c in consts])


def _row(v):
    return v.reshape(1, -1)


def _ffn(x, hist, ng, wg, wu, wdw, bdw, wd, fin, final_norm):
    body = functools.partial(_ffn_kernel, chunks=_col_chunks(D_FF, FFN_CHUNK), final_norm=final_norm)
    scratch = lambda nb, tt: [pltpu.VMEM((2, nb, FFN_PAD + tt, FFN_CHUNK), F32),
                              pltpu.VMEM((nb, FFN_PAD, D_FF), F32)]
    return _call(body, "conv_ffn", x, hist,
                 [_row(ng), wg, wu, wdw, _row(bdw), wd, _row(fin)], scratch, (FFN_WIDTH - 1, D_FF))


def _conf(x, hist, ng, w1, b1, wdw, bdw, lng, lnb, w2, b2):
    body = functools.partial(_conf_kernel, chunks=_col_chunks(D_MODEL, CONF_CHUNK))
    scratch = lambda nb, tt: [pltpu.VMEM((N_SLABS, nb, CONF_PAD + tt, LANES), F32),
                              pltpu.VMEM((N_SLABS, nb, tt, LANES), F32)]
    return _call(body, "conformer_conv", x, hist,
                 [_row(ng), w1, _row(b1), wdw, _row(bdw), _row(lng), _row(lnb), w2, _row(b2)],
                 scratch, (CONF_WIDTH - 1, D_MODEL))


def _pool(x, hist, ng, wp, scale, pos0):
    body = functools.partial(_pool_kernel, pos0=pos0)
    scratch = lambda nb, tt: [pltpu.VMEM((N_SLABS, nb, POOL_PAD + tt, LANES), F32),
                              pltpu.VMEM((N_SLABS, nb, tt, LANES), F32)]
    return _call(body, "pool_mixer", x, hist, [_row(ng), wp, _row(scale)], scratch,
                 (POOL_HIST, D_MODEL))


def _sconv(x, hist, ng, win, wdw, wout):
    body = functools.partial(_sconv_kernel, chunks=_col_chunks(D_MODEL, SCONV_CHUNK))
    scratch = lambda nb, tt: [pltpu.VMEM((nb, SCONV_PAD + tt, D_MODEL), F32)]
    return _call(body, "short_conv", x, hist, [_row(ng), win, wdw, wout], scratch,
                 (SCONV_WIDTH - 1, D_MODEL))


def _hgrn(x, s0, ng, w4, wo, lb_logits, hng, layer):
    body = functools.partial(_hgrn_kernel, layer=layer, group=HGRN_HEAD_GROUP)
    scratch = lambda nb, tt: [pltpu.VMEM((nb, HGRN_HEADS, HGRN_HEAD_DIM, HGRN_HEAD_DIM), F32),
                              pltpu.VMEM((nb * tt, D_MODEL), BF16)]
    return _call(body, "hgrn2", x, s0, [_row(ng), w4, wo, lb_logits, _row(hng)], scratch,
                 (HGRN_HEADS, HGRN_HEAD_DIM, HGRN_HEAD_DIM))


def _trunk(x, pos0, conf_hist, pool_hist, sconv_hist, hgrn_state, ffn_hist, p):
    depth = p['norm_mix'].shape[0]
    new = {'conf': [], 'pool': [], 'sconv': [], 'hgrn': [], 'ffn': []}
    for i in range(depth):
        mixer, j = i % 4, i // 4
        ng = p['norm_mix'][i]
        if mixer == 0:
            x, h = _conf(x, conf_hist[j], ng, p['conf_w_pw1'][j], p['conf_b_pw1'][j],
                         p['conf_w_dw'][j], p['conf_b_dw'][j], p['conf_ln_g'][j], p['conf_ln_b'][j],
                         p['conf_w_pw2'][j], p['conf_b_pw2'][j])
            new['conf'].append(h)
        elif mixer == 1:
            x, h = _pool(x, pool_hist[j], ng, p['pool_w'][j], p['pool_scale'][j], pos0)
            new['pool'].append(h)
        elif mixer == 2:
            x, h = _sconv(x, sconv_hist[j], ng, p['sconv_w_in'][j], p['sconv_w_dw'][j],
                          p['sconv_w_out'][j])
            new['sconv'].append(h)
        else:
            x, h = _hgrn(x, hgrn_state[j], ng, p['hgrn_w4'][j], p['hgrn_w_o'][j],
                         p['hgrn_lb_logits'], p['hgrn_norm_g'][j], i)
            new['hgrn'].append(h)
        x, h = _ffn(x, ffn_hist[i], p['norm_ffn'][i], _Layer(p['ffn_w_gate'], i),
                    _Layer(p['ffn_w_up'], i), p['ffn_w_dw'][i], p['ffn_b_dw'][i],
                    _Layer(p['ffn_w_down'], i), p['norm_final'],
                    final_norm=(i == depth - 1))
        new['ffn'].append(h)
    return (x,) + tuple(jnp.stack(new[k]) for k in ('conf', 'pool', 'sconv', 'hgrn', 'ffn'))


def kernel(x_prompt, x_sample, state_conformer_conv, state_pool, state_short_conv, state_hgrn, state_ffn_conv, norm_mix, norm_ffn, norm_final, conf_w_pw1, conf_b_pw1, conf_w_dw, conf_b_dw, conf_ln_g, conf_ln_b, conf_w_pw2, conf_b_pw2, pool_w, pool_scale, sconv_w_in, sconv_w_dw, sconv_w_out, hgrn_w_q, hgrn_w_f, hgrn_w_i, hgrn_w_g, hgrn_w_o, hgrn_lb_logits, hgrn_norm_g, ffn_w_gate, ffn_w_up, ffn_w_dw, ffn_b_dw, ffn_w_down):
    w4 = jnp.concatenate(
        [w[:, :, h * HGRN_HEAD_DIM:(h + 1) * HGRN_HEAD_DIM]
         for h in range(HGRN_HEADS) for w in (hgrn_w_q, hgrn_w_f, hgrn_w_i, hgrn_w_g)],
        axis=-1).astype(BF16)
    p = {
        'norm_mix': norm_mix, 'norm_ffn': norm_ffn, 'norm_final': norm_final,
        'conf_w_pw1': conf_w_pw1.astype(BF16), 'conf_b_pw1': conf_b_pw1, 'conf_w_dw': conf_w_dw,
        'conf_b_dw': conf_b_dw, 'conf_ln_g': conf_ln_g, 'conf_ln_b': conf_ln_b,
        'conf_w_pw2': conf_w_pw2.astype(BF16), 'conf_b_pw2': conf_b_pw2,
        'pool_w': pool_w.astype(BF16), 'pool_scale': pool_scale,
        'sconv_w_in': sconv_w_in.astype(BF16), 'sconv_w_dw': sconv_w_dw,
        'sconv_w_out': sconv_w_out.astype(BF16),
        'hgrn_w4': w4,
        'hgrn_w_o': hgrn_w_o.astype(BF16), 'hgrn_lb_logits': hgrn_lb_logits,
        'hgrn_norm_g': hgrn_norm_g,
        'ffn_w_gate': ffn_w_gate.astype(BF16), 'ffn_w_up': ffn_w_up.astype(BF16),
        'ffn_w_dw': ffn_w_dw, 'ffn_b_dw': ffn_b_dw, 'ffn_w_down': ffn_w_down.astype(BF16),
    }
    b = x_prompt.shape[0]
    zeros_like_state = lambda s: jnp.zeros((s.shape[0], b) + s.shape[2:], s.dtype)
    y_p, conf_p, pool_p, sconv_p, hgrn_p, ffn_p = _trunk(
        x_prompt, 0, zeros_like_state(state_conformer_conv), zeros_like_state(state_pool),
        zeros_like_state(state_short_conv), zeros_like_state(state_hgrn),
        zeros_like_state(state_ffn_conv), p)
    y_s, conf_s, pool_s, sconv_s, hgrn_s, ffn_s = _trunk(
        x_sample, PAST_LEN, state_conformer_conv, state_pool, state_short_conv, state_hgrn,
        state_ffn_conv, p)
    return (y_p, y_s, conf_p, conf_s, pool_p, pool_s, sconv_p, sconv_s,
            hgrn_p, hgrn_s, ffn_p, ffn_s)
```

```python
import functools
from typing import NamedTuple

import jax
import jax.numpy as jnp
from jax import lax
from jax.experimental import pallas as pl
from jax.experimental.pallas import tpu as pltpu

D_MODEL = 1024
D_FF = 2816
CONF_WIDTH = 31
POOL_WINDOWS = (2, 4, 8, 16)
POOL_GROUP_DIM = D_MODEL // len(POOL_WINDOWS)
POOL_HIST = max(POOL_WINDOWS) - 1
SCONV_WIDTH = 3
FFN_WIDTH = 3
HGRN_HEAD_DIM = 128
HGRN_HEADS = D_MODEL // HGRN_HEAD_DIM
HGRN_CHUNK = 64
PAST_LEN = 2048
EPS = 1e-6

SUBLANES = 8
LANES = 128
TILE_ROWS = 512
VMEM_LIMIT_BYTES = 56 * 1024 * 1024

F32 = jnp.float32
BF16 = jnp.bfloat16


def _round_up(n, m):
    return (n + m - 1) // m * m


def _rms(x, g):
    return x * lax.rsqrt(jnp.mean(x * x, axis=-1, keepdims=True) + EPS) * g


def _dot(a, b):
    return jnp.dot(a, b, preferred_element_type=F32)


def _dot_nt(a, b):
    return lax.dot_general(a, b, (((1,), (1,)), ((), ())), preferred_element_type=F32)


def _dot_tn(a, b):
    return lax.dot_general(a, b, (((0,), (0,)), ((), ())), preferred_element_type=F32)


def _col_chunks(total, width):
    return [(lo, min(lo + width, total)) for lo in range(0, total, width)]


FFN_PAD = SUBLANES
FFN_CHUNK = 768


def _ffn_kernel(x_ref, hist_ref, ng_ref, wg_ref, wu_ref, wdw_ref, bdw_ref, wd_ref, fin_ref,
                o_ref, nh_ref, cbuf, carry, *, nb, tt, chunks, final_norm):
    m = nb * tt
    h0 = FFN_PAD - (FFN_WIDTH - 1)

    @pl.when(pl.program_id(1) == 0)
    def _():
        carry[:, h0:FFN_PAD, :] = hist_ref[...]

    x = x_ref[...].reshape(m, D_MODEL)
    xn = _rms(x, ng_ref[...]).astype(BF16)
    acc = x

    def up_projections(lo, hi):
        return _dot(xn, wg_ref[:, lo:hi]), _dot(xn, wu_ref[:, lo:hi])

    projected = [up_projections(*c) for c in chunks]
    for ci, (lo, hi) in enumerate(chunks):
        nc = hi - lo
        g, u = projected[ci]
        g = g.reshape(nb, tt, nc)
        cb = cbuf.at[ci % 2]
        cb[:, h0:FFN_PAD, 0:nc] = carry[:, h0:FFN_PAD, lo:hi]
        cb[:, FFN_PAD:FFN_PAD + tt, 0:nc] = g
        g1 = cb[:, FFN_PAD - 1:FFN_PAD - 1 + tt, 0:nc]
        g2 = cb[:, FFN_PAD - 2:FFN_PAD - 2 + tt, 0:nc]
        w = wdw_ref[:, lo:hi]
        c = g * w[2:3] + g1 * w[1:2] + g2 * w[0:1] + bdw_ref[:, lo:hi]
        carry[:, h0:FFN_PAD, lo:hi] = cb[:, h0 + tt:FFN_PAD + tt, 0:nc]
        h = (c * jax.nn.sigmoid(c)).reshape(m, nc) * u
        acc = acc + _dot(h.astype(BF16), wd_ref[lo:hi, :])
    nh_ref[...] = carry[:, h0:FFN_PAD, :]
    if final_norm:
        acc = _rms(acc, fin_ref[...])
    o_ref[...] = acc.reshape(nb, tt, D_MODEL)


CONF_PAD = _round_up(CONF_WIDTH - 1, SUBLANES)
CONF_CHUNK = 512
ROW_STRIDE = 4
ROW_GROUP = ROW_STRIDE * SUBLANES
N_SLABS = D_MODEL // LANES


def _conf_kernel(x_ref, hist_ref, ng_ref, w1_ref, b1_ref, wdw_ref, bdw_ref, lng_ref, lnb_ref,
                 w2_ref, b2_ref, o_ref, nh_ref, ubuf, cbuf, *, nb, tt, chunks):
    m = nb * tt
    hw = CONF_WIDTH - 1
    h0 = CONF_PAD - hw

    @pl.when(pl.program_id(1) == 0)
    def _():
        for j in range(N_SLABS):
            ubuf[j, :, h0:CONF_PAD, :] = hist_ref[:, :, j * LANES:(j + 1) * LANES]

    x = x_ref[...].reshape(m, D_MODEL)
    xn = _rms(x, ng_ref[...]).astype(BF16)
    projected = [(_dot(xn, w1_ref[:, lo:hi]), _dot(xn, w1_ref[:, D_MODEL + lo:D_MODEL + hi]))
                 for lo, hi in chunks]
    for (lo, hi), (a, gt) in zip(chunks, projected):
        a = a + b1_ref[:, lo:hi]
        gt = gt + b1_ref[:, D_MODEL + lo:D_MODEL + hi]
        u = a * jax.nn.sigmoid(gt)
        for l0 in range(lo, hi, LANES):
            ubuf[l0 // LANES, :, CONF_PAD:CONF_PAD + tt, :] = (
                u[:, l0 - lo:l0 - lo + LANES].reshape(nb, tt, LANES))

    def slab(j, carry):
        l0 = pl.multiple_of(j * LANES, LANES)
        w = wdw_ref[:, pl.ds(l0, LANES)]
        wk = [jnp.broadcast_to(w[k:k + 1], (SUBLANES, LANES)) for k in range(CONF_WIDTH)]
        bias = jnp.broadcast_to(bdw_ref[:, pl.ds(l0, LANES)], (SUBLANES, LANES))
        for b in range(nb):
            for r0 in range(0, tt, ROW_GROUP):
                for ph in range(ROW_STRIDE):
                    acc = bias
                    for k in range(CONF_WIDTH):
                        rows = pl.ds(h0 + k + r0 + ph, SUBLANES, stride=ROW_STRIDE)
                        acc = acc + ubuf[j, b, rows, :] * wk[k]
                    cbuf[j, b, pl.ds(r0 + ph, SUBLANES, stride=ROW_STRIDE), :] = acc
        return carry

    lax.fori_loop(0, N_SLABS, slab, 0)

    for j in range(N_SLABS):
        nh_ref[:, :, j * LANES:(j + 1) * LANES] = ubuf[j, :, h0 + tt:CONF_PAD + tt, :]
        ubuf[j, :, h0:CONF_PAD, :] = ubuf[j, :, h0 + tt:CONF_PAD + tt, :]

    c = jnp.concatenate([cbuf[j].reshape(m, LANES) for j in range(N_SLABS)], axis=-1)
    mu = jnp.mean(c, axis=-1, keepdims=True)
    cc = c - mu
    var = jnp.mean(cc * cc, axis=-1, keepdims=True)
    cn = cc * lax.rsqrt(var + EPS) * lng_ref[...] + lnb_ref[...]
    s = cn * jax.nn.sigmoid(cn)
    y = _dot(s.astype(BF16), w2_ref[...]) + b2_ref[...]
    o_ref[...] = (x + y).reshape(nb, tt, D_MODEL)


POOL_PAD = _round_up(POOL_HIST, SUBLANES)


def _pool_kernel(x_ref, hist_ref, ng_ref, wp_ref, sc_ref, o_ref, nh_ref, pbuf, dbuf,
                 *, nb, tt, pos0):
    m = nb * tt
    h0 = POOL_PAD - POOL_HIST
    t = pl.program_id(1)

    @pl.when(t == 0)
    def _():
        for j in range(N_SLABS):
            pbuf[j, :, h0:POOL_PAD, :] = hist_ref[:, :, j * LANES:(j + 1) * LANES]

    x = x_ref[...].reshape(m, D_MODEL)
    xn = _rms(x, ng_ref[...])
    for j in range(N_SLABS):
        pbuf[j, :, POOL_PAD:POOL_PAD + tt, :] = xn[:, j * LANES:(j + 1) * LANES].reshape(nb, tt, LANES)

    tile_rows = ROW_STRIDE * lax.broadcasted_iota(jnp.int32, (SUBLANES, 1), 0)
    slabs_per_group = POOL_GROUP_DIM // LANES
    for gi, win in enumerate(POOL_WINDOWS):
        for r0 in range(0, tt, ROW_GROUP):
            for ph in range(ROW_STRIDE):
                pos = pos0 + t * tt + r0 + ph + tile_rows
                inv = 1.0 / jnp.minimum(pos + 1, win).astype(F32)
                for j in range(gi * slabs_per_group, (gi + 1) * slabs_per_group):
                    for b in range(nb):
                        cur = pbuf[j, b, pl.ds(POOL_PAD + r0 + ph, SUBLANES, stride=ROW_STRIDE), :]
                        acc = cur
                        for i in range(1, win):
                            rows = pl.ds(POOL_PAD + r0 + ph - i, SUBLANES, stride=ROW_STRIDE)
                            acc = acc + pbuf[j, b, rows, :]
                        dbuf[j, b, pl.ds(r0 + ph, SUBLANES, stride=ROW_STRIDE), :] = acc * inv - cur

    for j in range(N_SLABS):
        nh_ref[:, :, j * LANES:(j + 1) * LANES] = pbuf[j, :, h0 + tt:POOL_PAD + tt, :]
        pbuf[j, :, h0:POOL_PAD, :] = pbuf[j, :, h0 + tt:POOL_PAD + tt, :]

    for gi in range(len(POOL_WINDOWS)):
        lo, hi = gi * POOL_GROUP_DIM, (gi + 1) * POOL_GROUP_DIM
        d = jnp.concatenate([dbuf[j].reshape(m, LANES) for j in range(lo // LANES, hi // LANES)],
                            axis=-1).astype(BF16)
        y = _dot(d, wp_ref[gi]) * sc_ref[:, lo:hi]
        o_ref[:, :, lo:hi] = (x[:, lo:hi] + y).reshape(nb, tt, POOL_GROUP_DIM)


SCONV_PAD = SUBLANES
SCONV_CHUNK = 256


def _sconv_kernel(x_ref, hist_ref, ng_ref, win_ref, wdw_ref, wout_ref, o_ref, nh_ref, pbuf,
                  *, nb, tt, chunks):
    m = nb * tt
    h0 = SCONV_PAD - (SCONV_WIDTH - 1)

    @pl.when(pl.program_id(1) == 0)
    def _():
        pbuf[:, h0:SCONV_PAD, :] = hist_ref[...]

    x = x_ref[...].reshape(m, D_MODEL)
    xn = _rms(x, ng_ref[...]).astype(BF16)
    acc = x
    projected = [tuple(_dot(xn, win_ref[:, part * D_MODEL + lo:part * D_MODEL + hi])
                       for part in range(3)) for lo, hi in chunks]
    for (lo, hi), (bg, cg, v) in zip(chunks, projected):
        nc = hi - lo
        p = (cg * v).reshape(nb, tt, nc)
        pbuf[:, SCONV_PAD:SCONV_PAD + tt, lo:hi] = p
        p1 = pbuf[:, SCONV_PAD - 1:SCONV_PAD - 1 + tt, lo:hi]
        p2 = pbuf[:, SCONV_PAD - 2:SCONV_PAD - 2 + tt, lo:hi]
        w = wdw_ref[:, lo:hi]
        c = p * w[2:3] + p1 * w[1:2] + p2 * w[0:1]
        pbuf[:, h0:SCONV_PAD, lo:hi] = pbuf[:, h0 + tt:SCONV_PAD + tt, lo:hi]
        acc = acc + _dot((bg * c.reshape(m, nc)).astype(BF16), wout_ref[lo:hi, :])
    nh_ref[...] = pbuf[:, h0:SCONV_PAD, :]
    o_ref[...] = acc.reshape(nb, tt, D_MODEL)


HGRN_PROJ = 4 * HGRN_HEAD_DIM
HGRN_BLOCK = 64
HGRN_HEAD_GROUP = 8


def _hgrn_kernel(x_ref, s0_ref, ng_ref, w4_ref, wo_ref, lbl_ref, hng_ref, o_ref, sout_ref,
                 st_ref, onbuf, *, nb, tt, layer, group):
    m = nb * tt
    hd = HGRN_HEAD_DIM
    ck = HGRN_CHUNK
    t = pl.program_id(1)

    @pl.when(t == 0)
    def _():
        for b in range(nb):
            for h in range(HGRN_HEADS):
                st_ref[b, h] = s0_ref[b, h].T

    x = x_ref[...].reshape(m, D_MODEL)
    xn = _rms(x, ng_ref[...]).astype(BF16)

    lg = lbl_ref[...]
    e = jnp.exp(lg - jnp.max(lg, axis=0, keepdims=True))
    p = e / jnp.sum(e, axis=0, keepdims=True)
    lb_all = p[1:2]
    for i in range(2, layer + 1):
        lb_all = lb_all + p[i:i + 1]
    if layer == 0:
        lb_all = jnp.zeros_like(p[0:1])

    blk = HGRN_BLOCK
    row = lax.broadcasted_iota(jnp.int32, (blk, blk), 0)
    col = lax.broadcasted_iota(jnp.int32, (blk, blk), 1)
    ck_shift = ck.bit_length() - 1
    causal = (row >= col) & ((row >> ck_shift) == (col >> ck_shift))
    tri = causal.astype(BF16)
    hng = hng_ref[...]

    blocks = list(range(0, m, blk))
    chunk_starts = list(range(0, blk, ck))
    for g0 in range(0, HGRN_HEADS, group):
        heads = list(range(g0, g0 + group))
        q, k, v, gate, lf3 = {}, {}, {}, {}, {}
        for h in heads:
            z = _dot(xn, w4_ref[:, h * HGRN_PROJ:(h + 1) * HGRN_PROJ])
            lb = lb_all[:, h * hd:(h + 1) * hd]
            zq = z[:, 0:hd]
            q[h] = zq * jax.nn.sigmoid(zq)
            f = lb + (1.0 - lb) * jax.nn.sigmoid(z[:, hd:2 * hd])
            k[h] = 1.0 - f
            lf = jnp.log(f)
            v[h] = z[:, 2 * hd:3 * hd].astype(BF16)
            zg = z[:, 3 * hd:4 * hd]
            gate[h] = zg * jax.nn.sigmoid(zg)
            hi = lf.astype(BF16)
            r1 = lf - hi.astype(F32)
            mid = r1.astype(BF16)
            lo = (r1 - mid.astype(F32)).astype(BF16)
            lf3[h] = jnp.concatenate([hi, mid, lo], axis=1)
        units = [(h, r0) for h in heads for r0 in blocks]
        bc3 = {u: _dot(tri, lf3[u[0]][u[1]:u[1] + blk]) for u in units}
        qd, kd, kk, lasts = {}, {}, {}, {}
        for h, r0 in units:
            u = (h, r0)
            b3 = bc3[u]
            bc = b3[:, 0:hd] + b3[:, hd:2 * hd] + b3[:, 2 * hd:3 * hd]
            lasts[u] = [bc[c0 + ck - 1:c0 + ck] for c0 in chunk_starts]
            bl = jnp.concatenate([jnp.broadcast_to(l, (ck, hd)) for l in lasts[u]], axis=0)
            q_b = q[h][r0:r0 + blk]
            k_b = k[h][r0:r0 + blk]
            qd[u] = (q_b * jnp.exp(bc)).astype(BF16)
            kd[u] = (k_b * jnp.exp(-bc)).astype(BF16)
            kk[u] = (k_b * jnp.exp(bl - bc)).astype(BF16)
        sc = {u: jnp.where(causal, _dot_nt(qd[u], kd[u]), 0.0).astype(BF16) for u in units}
        o_intra = {u: _dot(sc[u], v[u[0]][u[1]:u[1] + blk]) for u in units}
        upd = {(h, r0, c0): _dot_tn(v[h][r0 + c0:r0 + c0 + ck], kk[(h, r0)][c0:c0 + ck])
               for h, r0 in units for c0 in chunk_starts}
        st_in = {}
        for h in heads:
            cur = {}
            for r0 in blocks:
                for ci, c0 in enumerate(chunk_starts):
                    b = (r0 + c0) // tt
                    if b not in cur:
                        cur[b] = st_ref[b, h]
                    st_in[(h, r0, c0)] = cur[b].astype(BF16)
                    cur[b] = cur[b] * jnp.exp(lasts[(h, r0)][ci]) + upd[(h, r0, c0)]
            for b, s in cur.items():
                st_ref[b, h] = s
        o_inter = {key: _dot_nt(qd[key[:2]][key[2]:key[2] + ck], st_in[key]) for key in st_in}
        for h, r0 in units:
            o = o_intra[(h, r0)] + jnp.concatenate([o_inter[(h, r0, c0)] for c0 in chunk_starts], axis=0)
            on = o * lax.rsqrt(jnp.mean(o * o, axis=-1, keepdims=True) + EPS) * hng
            onbuf[r0:r0 + blk, h * hd:(h + 1) * hd] = (on * gate[h][r0:r0 + blk]).astype(BF16)

    y = _dot(onbuf[...], wo_ref[...])
    o_ref[...] = (x + y).reshape(nb, tt, D_MODEL)

    @pl.when(t == pl.num_programs(1) - 1)
    def _():
        for b in range(nb):
            for h in range(HGRN_HEADS):
                sout_ref[b, h] = st_ref[b, h].T


def _tiling(batch, seq, rows):
    tt = min(seq, rows)
    nb = max(1, min(batch, rows // tt))
    assert seq % tt == 0 and batch % nb == 0
    assert tt % HGRN_CHUNK == 0 and tt % ROW_GROUP == 0 and (nb * tt) % HGRN_BLOCK == 0
    return nb, tt


class _Layer(NamedTuple):
    stacked: jax.Array
    index: int


def _operand(c):
    return c.stacked if isinstance(c, _Layer) else c


def _resident(c):
    if isinstance(c, _Layer):
        shape = c.stacked.shape[1:]
        index = (c.index,) + (0,) * len(shape)
        return pl.BlockSpec((None,) + shape, lambda b, t: index, pipeline_mode=pl.Buffered(1))
    zeros = (0,) * c.ndim
    return pl.BlockSpec(c.shape, lambda b, t: zeros, pipeline_mode=pl.Buffered(1))


def _x_spec(nb, tt):
    return pl.BlockSpec((nb, tt, D_MODEL), lambda b, t: (b, t, 0))


def _state_spec(nb, shape):
    zeros = (0,) * len(shape)
    return pl.BlockSpec((nb,) + shape, lambda b, t: (b,) + zeros)


def _call(body, name, x, state, consts, scratch, extra_out_shape, rows=TILE_ROWS):
    batch, seq, _ = x.shape
    nb, tt = _tiling(batch, seq, rows)
    state_shape = state.shape[1:]
    return pl.pallas_call(
        functools.partial(body, nb=nb, tt=tt),
        name=name,
        grid=(batch // nb, seq // tt),
        in_specs=[_x_spec(nb, tt), _state_spec(nb, state_shape)] + [_resident(c) for c in consts],
        out_specs=[_x_spec(nb, tt), _state_spec(nb, extra_out_shape)],
        out_shape=[jax.ShapeDtypeStruct(x.shape, F32),
                   jax.ShapeDtypeStruct((batch,) + extra_out_shape, F32)],
        scratch_shapes=scratch(nb, tt),
        compiler_params=pltpu.CompilerParams(
            dimension_semantics=("arbitrary", "arbitrary"),
            vmem_limit_bytes=VMEM_LIMIT_BYTES),
    )(x, state, *[_operand(c) for c in consts])


def _row(v):
    return v.reshape(1, -1)


def _ffn(x, hist, ng, wg, wu, wdw, bdw, wd, fin, final_norm):
    body = functools.partial(_ffn_kernel, chunks=_col_chunks(D_FF, FFN_CHUNK), final_norm=final_norm)
    scratch = lambda nb, tt: [pltpu.VMEM((2, nb, FFN_PAD + tt, FFN_CHUNK), F32),
                              pltpu.VMEM((nb, FFN_PAD, D_FF), F32)]
    return _call(body, "conv_ffn", x, hist,
                 [_row(ng), wg, wu, wdw, _row(bdw), wd, _row(fin)], scratch, (FFN_WIDTH - 1, D_FF))


def _conf(x, hist, ng, w1, b1, wdw, bdw, lng, lnb, w2, b2):
    body = functools.partial(_conf_kernel, chunks=_col_chunks(D_MODEL, CONF_CHUNK))
    scratch = lambda nb, tt: [pltpu.VMEM((N_SLABS, nb, CONF_PAD + tt, LANES), F32),
                              pltpu.VMEM((N_SLABS, nb, tt, LANES), F32)]
    return _call(body, "conformer_conv", x, hist,
                 [_row(ng), w1, _row(b1), wdw, _row(bdw), _row(lng), _row(lnb), w2, _row(b2)],
                 scratch, (CONF_WIDTH - 1, D_MODEL), rows=1024)


def _pool(x, hist, ng, wp, scale, pos0):
    body = functools.partial(_pool_kernel, pos0=pos0)
    scratch = lambda nb, tt: [pltpu.VMEM((N_SLABS, nb, POOL_PAD + tt, LANES), F32),
                              pltpu.VMEM((N_SLABS, nb, tt, LANES), F32)]
    return _call(body, "pool_mixer", x, hist, [_row(ng), wp, _row(scale)], scratch,
                 (POOL_HIST, D_MODEL), rows=1024)


def _sconv(x, hist, ng, win, wdw, wout):
    body = functools.partial(_sconv_kernel, chunks=_col_chunks(D_MODEL, SCONV_CHUNK))
    scratch = lambda nb, tt: [pltpu.VMEM((nb, SCONV_PAD + tt, D_MODEL), F32)]
    return _call(body, "short_conv", x, hist, [_row(ng), win, wdw, wout], scratch,
                 (SCONV_WIDTH - 1, D_MODEL), rows=1024)


def _hgrn(x, s0, ng, w4, wo, lb_logits, hng, layer):
    body = functools.partial(_hgrn_kernel, layer=layer, group=HGRN_HEAD_GROUP)
    scratch = lambda nb, tt: [pltpu.VMEM((nb, HGRN_HEADS, HGRN_HEAD_DIM, HGRN_HEAD_DIM), F32),
                              pltpu.VMEM((nb * tt, D_MODEL), BF16)]
    return _call(body, "hgrn2", x, s0, [_row(ng), w4, wo, lb_logits, _row(hng)], scratch,
                 (HGRN_HEADS, HGRN_HEAD_DIM, HGRN_HEAD_DIM))


def _trunk(x, pos0, conf_hist, pool_hist, sconv_hist, hgrn_state, ffn_hist, p):
    depth = p['norm_mix'].shape[0]
    new = {'conf': [], 'pool': [], 'sconv': [], 'hgrn': [], 'ffn': []}
    for i in range(depth):
        mixer, j = i % 4, i // 4
        ng = p['norm_mix'][i]
        if mixer == 0:
            x, h = _conf(x, conf_hist[j], ng, p['conf_w_pw1'][j], p['conf_b_pw1'][j],
                         p['conf_w_dw'][j], p['conf_b_dw'][j], p['conf_ln_g'][j], p['conf_ln_b'][j],
                         p['conf_w_pw2'][j], p['conf_b_pw2'][j])
            new['conf'].append(h)
        elif mixer == 1:
            x, h = _pool(x, pool_hist[j], ng, p['pool_w'][j], p['pool_scale'][j], pos0)
            new['pool'].append(h)
        elif mixer == 2:
            x, h = _sconv(x, sconv_hist[j], ng, p['sconv_w_in'][j], p['sconv_w_dw'][j],
                          p['sconv_w_out'][j])
            new['sconv'].append(h)
        else:
            x, h = _hgrn(x, hgrn_state[j], ng, p['hgrn_w4'][j], p['hgrn_w_o'][j],
                         p['hgrn_lb_logits'], p['hgrn_norm_g'][j], i)
            new['hgrn'].append(h)
        x, h = _ffn(x, ffn_hist[i], p['norm_ffn'][i], _Layer(p['ffn_w_gate'], i),
                    _Layer(p['ffn_w_up'], i), p['ffn_w_dw'][i], p['ffn_b_dw'][i],
                    _Layer(p['ffn_w_down'], i), p['norm_final'],
                    final_norm=(i == depth - 1))
        new['ffn'].append(h)
    return (x,) + tuple(jnp.stack(new[k]) for k in ('conf', 'pool', 'sconv', 'hgrn', 'ffn'))


def kernel(x_prompt, x_sample, state_conformer_conv, state_pool, state_short_conv, state_hgrn, state_ffn_conv, norm_mix, norm_ffn, norm_final, conf_w_pw1, conf_b_pw1, conf_w_dw, conf_b_dw, conf_ln_g, conf_ln_b, conf_w_pw2, conf_b_pw2, pool_w, pool_scale, sconv_w_in, sconv_w_dw, sconv_w_out, hgrn_w_q, hgrn_w_f, hgrn_w_i, hgrn_w_g, hgrn_w_o, hgrn_lb_logits, hgrn_norm_g, ffn_w_gate, ffn_w_up, ffn_w_dw, ffn_b_dw, ffn_w_down):
    w4 = jnp.concatenate(
        [w[:, :, h * HGRN_HEAD_DIM:(h + 1) * HGRN_HEAD_DIM]
         for h in range(HGRN_HEADS) for w in (hgrn_w_q, hgrn_w_f, hgrn_w_i, hgrn_w_g)],
        axis=-1).astype(BF16)
    p = {
        'norm_mix': norm_mix, 'norm_ffn': norm_ffn, 'norm_final': norm_final,
        'conf_w_pw1': conf_w_pw1.astype(BF16), 'conf_b_pw1': conf_b_pw1, 'conf_w_dw': conf_w_dw,
        'conf_b_dw': conf_b_dw, 'conf_ln_g': conf_ln_g, 'conf_ln_b': conf_ln_b,
        'conf_w_pw2': conf_w_pw2.astype(BF16), 'conf_b_pw2': conf_b_pw2,
        'pool_w': pool_w.astype(BF16), 'pool_scale': pool_scale,
        'sconv_w_in': sconv_w_in.astype(BF16), 'sconv_w_dw': sconv_w_dw,
        'sconv_w_out': sconv_w_out.astype(BF16),
        'hgrn_w4': w4,
        'hgrn_w_o': hgrn_w_o.astype(BF16), 'hgrn_lb_logits': hgrn_lb_logits,
        'hgrn_norm_g': hgrn_norm_g,
        'ffn_w_gate': ffn_w_gate.astype(BF16), 'ffn_w_up': ffn_w_up.astype(BF16),
        'ffn_w_dw': ffn_w_dw, 'ffn_b_dw': ffn_b_dw, 'ffn_w_down': ffn_w_down.astype(BF16),
    }
    b = x_prompt.shape[0]
    zeros_like_state = lambda s: jnp.zeros((s.shape[0], b) + s.shape[2:], s.dtype)
    y_p, conf_p, pool_p, sconv_p, hgrn_p, ffn_p = _trunk(
        x_prompt, 0, zeros_like_state(state_conformer_conv), zeros_like_state(state_pool),
        zeros_like_state(state_short_conv), zeros_like_state(state_hgrn),
        zeros_like_state(state_ffn_conv), p)
    y_s, conf_s, pool_s, sconv_s, hgrn_s, ffn_s = _trunk(
        x_sample, PAST_LEN, state_conformer_conv, state_pool, state_short_conv, state_hgrn,
        state_ffn_conv, p)
    return (y_p, y_s, conf_p, conf_s, pool_p, pool_s, sconv_p, sconv_s,
            hgrn_p, hgrn_s, ffn_p, ffn_s)
```

```python
import functools
from typing import NamedTuple

import jax
import jax.numpy as jnp
from jax import lax
from jax.experimental import pallas as pl
from jax.experimental.pallas import tpu as pltpu

D_MODEL = 1024
D_FF = 2816
CONF_WIDTH = 31
POOL_WINDOWS = (2, 4, 8, 16)
POOL_GROUP_DIM = D_MODEL // len(POOL_WINDOWS)
POOL_HIST = max(POOL_WINDOWS) - 1
SCONV_WIDTH = 3
FFN_WIDTH = 3
HGRN_HEAD_DIM = 128
HGRN_HEADS = D_MODEL // HGRN_HEAD_DIM
HGRN_CHUNK = 64
PAST_LEN = 2048
EPS = 1e-6

SUBLANES = 8
LANES = 128
TILE_ROWS = 512
LIGHT_MIXER_TILE_ROWS = 1024
VMEM_LIMIT_BYTES = 56 * 1024 * 1024

F32 = jnp.float32
BF16 = jnp.bfloat16


def _round_up(n, m):
    return (n + m - 1) // m * m


def _rms(x, g):
    return x * lax.rsqrt(jnp.mean(x * x, axis=-1, keepdims=True) + EPS) * g


def _dot(a, b):
    return jnp.dot(a, b, preferred_element_type=F32)


def _dot_nt(a, b):
    return lax.dot_general(a, b, (((1,), (1,)), ((), ())), preferred_element_type=F32)


def _dot_tn(a, b):
    return lax.dot_general(a, b, (((0,), (0,)), ((), ())), preferred_element_type=F32)


def _col_chunks(total, width):
    return [(lo, min(lo + width, total)) for lo in range(0, total, width)]


FFN_PAD = SUBLANES
FFN_CHUNK = 768


def _ffn_kernel(x_ref, hist_ref, ng_ref, wg_ref, wu_ref, wdw_ref, bdw_ref, wd_ref, fin_ref,
                o_ref, nh_ref, cbuf, carry, *, nb, tt, chunks, final_norm):
    m = nb * tt
    h0 = FFN_PAD - (FFN_WIDTH - 1)

    @pl.when(pl.program_id(1) == 0)
    def _():
        carry[:, h0:FFN_PAD, :] = hist_ref[...]

    x = x_ref[...].reshape(m, D_MODEL)
    xn = _rms(x, ng_ref[...]).astype(BF16)
    acc = x

    def up_projections(lo, hi):
        return _dot(xn, wg_ref[:, lo:hi]), _dot(xn, wu_ref[:, lo:hi])

    projected = [up_projections(*c) for c in chunks]
    for ci, (lo, hi) in enumerate(chunks):
        nc = hi - lo
        g, u = projected[ci]
        g = g.reshape(nb, tt, nc)
        cb = cbuf.at[ci % 2]
        cb[:, h0:FFN_PAD, 0:nc] = carry[:, h0:FFN_PAD, lo:hi]
        cb[:, FFN_PAD:FFN_PAD + tt, 0:nc] = g
        g1 = cb[:, FFN_PAD - 1:FFN_PAD - 1 + tt, 0:nc]
        g2 = cb[:, FFN_PAD - 2:FFN_PAD - 2 + tt, 0:nc]
        w = wdw_ref[:, lo:hi]
        c = g * w[2:3] + g1 * w[1:2] + g2 * w[0:1] + bdw_ref[:, lo:hi]
        carry[:, h0:FFN_PAD, lo:hi] = cb[:, h0 + tt:FFN_PAD + tt, 0:nc]
        h = (c * jax.nn.sigmoid(c)).reshape(m, nc) * u
        acc = acc + _dot(h.astype(BF16), wd_ref[lo:hi, :])
    nh_ref[...] = carry[:, h0:FFN_PAD, :]
    if final_norm:
        acc = _rms(acc, fin_ref[...])
    o_ref[...] = acc.reshape(nb, tt, D_MODEL)


CONF_PAD = _round_up(CONF_WIDTH - 1, SUBLANES)
CONF_CHUNK = 256
ROW_STRIDE = 4
ROW_GROUP = ROW_STRIDE * SUBLANES
N_SLABS = D_MODEL // LANES


def _conf_kernel(x_ref, hist_ref, ng_ref, w1_ref, b1_ref, wdw_ref, bdw_ref, lng_ref, lnb_ref,
                 w2_ref, b2_ref, o_ref, nh_ref, ubuf, cbuf, *, nb, tt, chunks):
    m = nb * tt
    hw = CONF_WIDTH - 1
    h0 = CONF_PAD - hw

    @pl.when(pl.program_id(1) == 0)
    def _():
        for j in range(N_SLABS):
            ubuf[j, :, h0:CONF_PAD, :] = hist_ref[:, :, j * LANES:(j + 1) * LANES]

    x = x_ref[...].reshape(m, D_MODEL)
    xn = _rms(x, ng_ref[...]).astype(BF16)
    projected = [(_dot(xn, w1_ref[:, lo:hi]), _dot(xn, w1_ref[:, D_MODEL + lo:D_MODEL + hi]))
                 for lo, hi in chunks]
    for (lo, hi), (a, gt) in zip(chunks, projected):
        a = a + b1_ref[:, lo:hi]
        gt = gt + b1_ref[:, D_MODEL + lo:D_MODEL + hi]
        u = a * jax.nn.sigmoid(gt)
        for l0 in range(lo, hi, LANES):
            ubuf[l0 // LANES, :, CONF_PAD:CONF_PAD + tt, :] = (
                u[:, l0 - lo:l0 - lo + LANES].reshape(nb, tt, LANES))

    def slab(j, carry):
        l0 = pl.multiple_of(j * LANES, LANES)
        w = wdw_ref[:, pl.ds(l0, LANES)]
        wk = [jnp.broadcast_to(w[k:k + 1], (SUBLANES, LANES)) for k in range(CONF_WIDTH)]
        bias = jnp.broadcast_to(bdw_ref[:, pl.ds(l0, LANES)], (SUBLANES, LANES))
        for b in range(nb):
            for r0 in range(0, tt, ROW_GROUP):
                for ph in range(ROW_STRIDE):
                    acc = bias
                    for k in range(CONF_WIDTH):
                        rows = pl.ds(h0 + k + r0 + ph, SUBLANES, stride=ROW_STRIDE)
                        acc = acc + ubuf[j, b, rows, :] * wk[k]
                    cbuf[j, b, pl.ds(r0 + ph, SUBLANES, stride=ROW_STRIDE), :] = acc
        return carry

    lax.fori_loop(0, N_SLABS, slab, 0)

    for j in range(N_SLABS):
        nh_ref[:, :, j * LANES:(j + 1) * LANES] = ubuf[j, :, h0 + tt:CONF_PAD + tt, :]
        ubuf[j, :, h0:CONF_PAD, :] = ubuf[j, :, h0 + tt:CONF_PAD + tt, :]

    c = jnp.concatenate([cbuf[j].reshape(m, LANES) for j in range(N_SLABS)], axis=-1)
    mu = jnp.mean(c, axis=-1, keepdims=True)
    cc = c - mu
    var = jnp.mean(cc * cc, axis=-1, keepdims=True)
    cn = cc * lax.rsqrt(var + EPS) * lng_ref[...] + lnb_ref[...]
    s = cn * jax.nn.sigmoid(cn)
    y = _dot(s.astype(BF16), w2_ref[...]) + b2_ref[...]
    o_ref[...] = (x + y).reshape(nb, tt, D_MODEL)


POOL_PAD = _round_up(POOL_HIST, SUBLANES)


def _pool_kernel(x_ref, hist_ref, ng_ref, wp_ref, sc_ref, o_ref, nh_ref, pbuf, dbuf,
                 *, nb, tt, pos0):
    m = nb * tt
    h0 = POOL_PAD - POOL_HIST
    t = pl.program_id(1)

    @pl.when(t == 0)
    def _():
        for j in range(N_SLABS):
            pbuf[j, :, h0:POOL_PAD, :] = hist_ref[:, :, j * LANES:(j + 1) * LANES]

    x = x_ref[...].reshape(m, D_MODEL)
    xn = _rms(x, ng_ref[...])
    for j in range(N_SLABS):
        pbuf[j, :, POOL_PAD:POOL_PAD + tt, :] = xn[:, j * LANES:(j + 1) * LANES].reshape(nb, tt, LANES)

    tile_rows = ROW_STRIDE * lax.broadcasted_iota(jnp.int32, (SUBLANES, 1), 0)
    slabs_per_group = POOL_GROUP_DIM // LANES
    for gi, win in enumerate(POOL_WINDOWS):
        for r0 in range(0, tt, ROW_GROUP):
            for ph in range(ROW_STRIDE):
                pos = pos0 + t * tt + r0 + ph + tile_rows
                inv = 1.0 / jnp.minimum(pos + 1, win).astype(F32)
                for j in range(gi * slabs_per_group, (gi + 1) * slabs_per_group):
                    for b in range(nb):
                        cur = pbuf[j, b, pl.ds(POOL_PAD + r0 + ph, SUBLANES, stride=ROW_STRIDE), :]
                        acc = cur
                        for i in range(1, win):
                            rows = pl.ds(POOL_PAD + r0 + ph - i, SUBLANES, stride=ROW_STRIDE)
                            acc = acc + pbuf[j, b, rows, :]
                        dbuf[j, b, pl.ds(r0 + ph, SUBLANES, stride=ROW_STRIDE), :] = acc * inv - cur

    for j in range(N_SLABS):
        nh_ref[:, :, j * LANES:(j + 1) * LANES] = pbuf[j, :, h0 + tt:POOL_PAD + tt, :]
        pbuf[j, :, h0:POOL_PAD, :] = pbuf[j, :, h0 + tt:POOL_PAD + tt, :]

    for gi in range(len(POOL_WINDOWS)):
        lo, hi = gi * POOL_GROUP_DIM, (gi + 1) * POOL_GROUP_DIM
        d = jnp.concatenate([dbuf[j].reshape(m, LANES) for j in range(lo // LANES, hi // LANES)],
                            axis=-1).astype(BF16)
        y = _dot(d, wp_ref[gi]) * sc_ref[:, lo:hi]
        o_ref[:, :, lo:hi] = (x[:, lo:hi] + y).reshape(nb, tt, POOL_GROUP_DIM)


SCONV_PAD = SUBLANES
SCONV_CHUNK = 256


def _sconv_kernel(x_ref, hist_ref, ng_ref, win_ref, wdw_ref, wout_ref, o_ref, nh_ref, pbuf,
                  *, nb, tt, chunks):
    m = nb * tt
    h0 = SCONV_PAD - (SCONV_WIDTH - 1)

    @pl.when(pl.program_id(1) == 0)
    def _():
        pbuf[:, h0:SCONV_PAD, :] = hist_ref[...]

    x = x_ref[...].reshape(m, D_MODEL)
    xn = _rms(x, ng_ref[...]).astype(BF16)
    acc = x
    projected = [tuple(_dot(xn, win_ref[:, part * D_MODEL + lo:part * D_MODEL + hi])
                       for part in range(3)) for lo, hi in chunks]
    for (lo, hi), (bg, cg, v) in zip(chunks, projected):
        nc = hi - lo
        p = (cg * v).reshape(nb, tt, nc)
        pbuf[:, SCONV_PAD:SCONV_PAD + tt, lo:hi] = p
        p1 = pbuf[:, SCONV_PAD - 1:SCONV_PAD - 1 + tt, lo:hi]
        p2 = pbuf[:, SCONV_PAD - 2:SCONV_PAD - 2 + tt, lo:hi]
        w = wdw_ref[:, lo:hi]
        c = p * w[2:3] + p1 * w[1:2] + p2 * w[0:1]
        pbuf[:, h0:SCONV_PAD, lo:hi] = pbuf[:, h0 + tt:SCONV_PAD + tt, lo:hi]
        acc = acc + _dot((bg * c.reshape(m, nc)).astype(BF16), wout_ref[lo:hi, :])
    nh_ref[...] = pbuf[:, h0:SCONV_PAD, :]
    o_ref[...] = acc.reshape(nb, tt, D_MODEL)


HGRN_PROJ = 4 * HGRN_HEAD_DIM
HGRN_BLOCK = 64
HGRN_HEAD_GROUP = 8


def _hgrn_kernel(x_ref, s0_ref, ng_ref, w4_ref, wo_ref, lbl_ref, hng_ref, o_ref, sout_ref,
                 st_ref, onbuf, *, nb, tt, layer, group):
    m = nb * tt
    hd = HGRN_HEAD_DIM
    ck = HGRN_CHUNK
    t = pl.program_id(1)

    @pl.when(t == 0)
    def _():
        for b in range(nb):
            for h in range(HGRN_HEADS):
                st_ref[b, h] = s0_ref[b, h].T

    x = x_ref[...].reshape(m, D_MODEL)
    xn = _rms(x, ng_ref[...]).astype(BF16)

    lg = lbl_ref[...]
    e = jnp.exp(lg - jnp.max(lg, axis=0, keepdims=True))
    p = e / jnp.sum(e, axis=0, keepdims=True)
    lb_all = p[1:2]
    for i in range(2, layer + 1):
        lb_all = lb_all + p[i:i + 1]
    if layer == 0:
        lb_all = jnp.zeros_like(p[0:1])

    blk = HGRN_BLOCK
    row = lax.broadcasted_iota(jnp.int32, (blk, blk), 0)
    col = lax.broadcasted_iota(jnp.int32, (blk, blk), 1)
    ck_shift = ck.bit_length() - 1
    causal = (row >= col) & ((row >> ck_shift) == (col >> ck_shift))
    tri = causal.astype(BF16)
    hng = hng_ref[...]

    blocks = list(range(0, m, blk))
    chunk_starts = list(range(0, blk, ck))
    for g0 in range(0, HGRN_HEADS, group):
        heads = list(range(g0, g0 + group))
        q, k, v, gate, lf3 = {}, {}, {}, {}, {}
        for h in heads:
            z = _dot(xn, w4_ref[:, h * HGRN_PROJ:(h + 1) * HGRN_PROJ])
            lb = lb_all[:, h * hd:(h + 1) * hd]
            zq = z[:, 0:hd]
            q[h] = zq * jax.nn.sigmoid(zq)
            f = lb + (1.0 - lb) * jax.nn.sigmoid(z[:, hd:2 * hd])
            k[h] = 1.0 - f
            lf = jnp.log(f)
            v[h] = z[:, 2 * hd:3 * hd].astype(BF16)
            zg = z[:, 3 * hd:4 * hd]
            gate[h] = zg * jax.nn.sigmoid(zg)
            hi = lf.astype(BF16)
            r1 = lf - hi.astype(F32)
            mid = r1.astype(BF16)
            lo = (r1 - mid.astype(F32)).astype(BF16)
            lf3[h] = jnp.concatenate([hi, mid, lo], axis=1)
        units = [(h, r0) for h in heads for r0 in blocks]
        bc3 = {u: _dot(tri, lf3[u[0]][u[1]:u[1] + blk]) for u in units}
        qd, kd, kk, lasts = {}, {}, {}, {}
        for h, r0 in units:
            u = (h, r0)
            b3 = bc3[u]
            bc = b3[:, 0:hd] + b3[:, hd:2 * hd] + b3[:, 2 * hd:3 * hd]
            lasts[u] = [bc[c0 + ck - 1:c0 + ck] for c0 in chunk_starts]
            bl = jnp.concatenate([jnp.broadcast_to(l, (ck, hd)) for l in lasts[u]], axis=0)
            q_b = q[h][r0:r0 + blk]
            k_b = k[h][r0:r0 + blk]
            qd[u] = (q_b * jnp.exp(bc)).astype(BF16)
            kd[u] = (k_b * jnp.exp(-bc)).astype(BF16)
            kk[u] = (k_b * jnp.exp(bl - bc)).astype(BF16)
        sc = {u: jnp.where(causal, _dot_nt(qd[u], kd[u]), 0.0).astype(BF16) for u in units}
        o_intra = {u: _dot(sc[u], v[u[0]][u[1]:u[1] + blk]) for u in units}
        upd = {(h, r0, c0): _dot_tn(v[h][r0 + c0:r0 + c0 + ck], kk[(h, r0)][c0:c0 + ck])
               for h, r0 in units for c0 in chunk_starts}
        st_in = {}
        for h in heads:
            cur = {}
            for r0 in blocks:
                for ci, c0 in enumerate(chunk_starts):
                    b = (r0 + c0) // tt
                    if b not in cur:
                        cur[b] = st_ref[b, h]
                    st_in[(h, r0, c0)] = cur[b].astype(BF16)
                    cur[b] = cur[b] * jnp.exp(lasts[(h, r0)][ci]) + upd[(h, r0, c0)]
            for b, s in cur.items():
                st_ref[b, h] = s
        o_inter = {key: _dot_nt(qd[key[:2]][key[2]:key[2] + ck], st_in[key]) for key in st_in}
        for h, r0 in units:
            o = o_intra[(h, r0)] + jnp.concatenate([o_inter[(h, r0, c0)] for c0 in chunk_starts], axis=0)
            on = o * lax.rsqrt(jnp.mean(o * o, axis=-1, keepdims=True) + EPS) * hng
            onbuf[r0:r0 + blk, h * hd:(h + 1) * hd] = (on * gate[h][r0:r0 + blk]).astype(BF16)

    y = _dot(onbuf[...], wo_ref[...])
    o_ref[...] = (x + y).reshape(nb, tt, D_MODEL)

    @pl.when(t == pl.num_programs(1) - 1)
    def _():
        for b in range(nb):
            for h in range(HGRN_HEADS):
                sout_ref[b, h] = st_ref[b, h].T


def _tiling(batch, seq, rows):
    tt = min(seq, rows)
    nb = max(1, min(batch, rows // tt))
    assert seq % tt == 0 and batch % nb == 0
    assert tt % HGRN_CHUNK == 0 and tt % ROW_GROUP == 0 and (nb * tt) % HGRN_BLOCK == 0
    return nb, tt


class _Layer(NamedTuple):
    stacked: jax.Array
    index: int


def _operand(c):
    return c.stacked if isinstance(c, _Layer) else c


def _resident(c):
    if isinstance(c, _Layer):
        shape = c.stacked.shape[1:]
        index = (c.index,) + (0,) * len(shape)
        return pl.BlockSpec((None,) + shape, lambda b, t: index, pipeline_mode=pl.Buffered(1))
    zeros = (0,) * c.ndim
    return pl.BlockSpec(c.shape, lambda b, t: zeros, pipeline_mode=pl.Buffered(1))


def _x_spec(nb, tt):
    return pl.BlockSpec((nb, tt, D_MODEL), lambda b, t: (b, t, 0))


def _state_spec(nb, shape):
    zeros = (0,) * len(shape)
    return pl.BlockSpec((nb,) + shape, lambda b, t: (b,) + zeros)


def _call(body, name, x, state, consts, scratch, extra_out_shape, rows=TILE_ROWS):
    batch, seq, _ = x.shape
    nb, tt = _tiling(batch, seq, rows)
    state_shape = state.shape[1:]
    return pl.pallas_call(
        functools.partial(body, nb=nb, tt=tt),
        name=name,
        grid=(batch // nb, seq // tt),
        in_specs=[_x_spec(nb, tt), _state_spec(nb, state_shape)] + [_resident(c) for c in consts],
        out_specs=[_x_spec(nb, tt), _state_spec(nb, extra_out_shape)],
        out_shape=[jax.ShapeDtypeStruct(x.shape, F32),
                   jax.ShapeDtypeStruct((batch,) + extra_out_shape, F32)],
        scratch_shapes=scratch(nb, tt),
        compiler_params=pltpu.CompilerParams(
            dimension_semantics=("arbitrary", "arbitrary"),
            vmem_limit_bytes=VMEM_LIMIT_BYTES),
    )(x, state, *[_operand(c) for c in consts])


def _row(v):
    return v.reshape(1, -1)


def _ffn(x, hist, ng, wg, wu, wdw, bdw, wd, fin, final_norm):
    body = functools.partial(_ffn_kernel, chunks=_col_chunks(D_FF, FFN_CHUNK), final_norm=final_norm)
    scratch = lambda nb, tt: [pltpu.VMEM((2, nb, FFN_PAD + tt, FFN_CHUNK), F32),
                              pltpu.VMEM((nb, FFN_PAD, D_FF), F32)]
    return _call(body, "conv_ffn", x, hist,
                 [_row(ng), wg, wu, wdw, _row(bdw), wd, _row(fin)], scratch, (FFN_WIDTH - 1, D_FF))


def _conf(x, hist, ng, w1, b1, wdw, bdw, lng, lnb, w2, b2):
    body = functools.partial(_conf_kernel, chunks=_col_chunks(D_MODEL, CONF_CHUNK))
    scratch = lambda nb, tt: [pltpu.VMEM((N_SLABS, nb, CONF_PAD + tt, LANES), F32),
                              pltpu.VMEM((N_SLABS, nb, tt, LANES), F32)]
    return _call(body, "conformer_conv", x, hist,
                 [_row(ng), w1, _row(b1), wdw, _row(bdw), _row(lng), _row(lnb), w2, _row(b2)],
                 scratch, (CONF_WIDTH - 1, D_MODEL), rows=LIGHT_MIXER_TILE_ROWS)


def _pool(x, hist, ng, wp, scale, pos0):
    body = functools.partial(_pool_kernel, pos0=pos0)
    scratch = lambda nb, tt: [pltpu.VMEM((N_SLABS, nb, POOL_PAD + tt, LANES), F32),
                              pltpu.VMEM((N_SLABS, nb, tt, LANES), F32)]
    return _call(body, "pool_mixer", x, hist, [_row(ng), wp, _row(scale)], scratch,
                 (POOL_HIST, D_MODEL), rows=LIGHT_MIXER_TILE_ROWS)


def _sconv(x, hist, ng, win, wdw, wout):
    body = functools.partial(_sconv_kernel, chunks=_col_chunks(D_MODEL, SCONV_CHUNK))
    scratch = lambda nb, tt: [pltpu.VMEM((nb, SCONV_PAD + tt, D_MODEL), F32)]
    return _call(body, "short_conv", x, hist, [_row(ng), win, wdw, wout], scratch,
                 (SCONV_WIDTH - 1, D_MODEL), rows=LIGHT_MIXER_TILE_ROWS)


def _hgrn(x, s0, ng, w4, wo, lb_logits, hng, layer):
    body = functools.partial(_hgrn_kernel, layer=layer, group=HGRN_HEAD_GROUP)
    scratch = lambda nb, tt: [pltpu.VMEM((nb, HGRN_HEADS, HGRN_HEAD_DIM, HGRN_HEAD_DIM), F32),
                              pltpu.VMEM((nb * tt, D_MODEL), BF16)]
    return _call(body, "hgrn2", x, s0, [_row(ng), w4, wo, lb_logits, _row(hng)], scratch,
                 (HGRN_HEADS, HGRN_HEAD_DIM, HGRN_HEAD_DIM))


def _trunk(x, pos0, conf_hist, pool_hist, sconv_hist, hgrn_state, ffn_hist, p):
    depth = p['norm_mix'].shape[0]
    new = {'conf': [], 'pool': [], 'sconv': [], 'hgrn': [], 'ffn': []}
    for i in range(depth):
        mixer, j = i % 4, i // 4
        ng = p['norm_mix'][i]
        if mixer == 0:
            x, h = _conf(x, conf_hist[j], ng, p['conf_w_pw1'][j], p['conf_b_pw1'][j],
                         p['conf_w_dw'][j], p['conf_b_dw'][j], p['conf_ln_g'][j], p['conf_ln_b'][j],
                         p['conf_w_pw2'][j], p['conf_b_pw2'][j])
            new['conf'].append(h)
        elif mixer == 1:
            x, h = _pool(x, pool_hist[j], ng, p['pool_w'][j], p['pool_scale'][j], pos0)
            new['pool'].append(h)
        elif mixer == 2:
            x, h = _sconv(x, sconv_hist[j], ng, p['sconv_w_in'][j], p['sconv_w_dw'][j],
                          p['sconv_w_out'][j])
            new['sconv'].append(h)
        else:
            x, h = _hgrn(x, hgrn_state[j], ng, p['hgrn_w4'][j], p['hgrn_w_o'][j],
                         p['hgrn_lb_logits'], p['hgrn_norm_g'][j], i)
            new['hgrn'].append(h)
        x, h = _ffn(x, ffn_hist[i], p['norm_ffn'][i], _Layer(p['ffn_w_gate'], i),
                    _Layer(p['ffn_w_up'], i), p['ffn_w_dw'][i], p['ffn_b_dw'][i],
                    _Layer(p['ffn_w_down'], i), p['norm_final'],
                    final_norm=(i == depth - 1))
        new['ffn'].append(h)
    return (x,) + tuple(jnp.stack(new[k]) for k in ('conf', 'pool', 'sconv', 'hgrn', 'ffn'))


def kernel(x_prompt, x_sample, state_conformer_conv, state_pool, state_short_conv, state_hgrn, state_ffn_conv, norm_mix, norm_ffn, norm_final, conf_w_pw1, conf_b_pw1, conf_w_dw, conf_b_dw, conf_ln_g, conf_ln_b, conf_w_pw2, conf_b_pw2, pool_w, pool_scale, sconv_w_in, sconv_w_dw, sconv_w_out, hgrn_w_q, hgrn_w_f, hgrn_w_i, hgrn_w_g, hgrn_w_o, hgrn_lb_logits, hgrn_norm_g, ffn_w_gate, ffn_w_up, ffn_w_dw, ffn_b_dw, ffn_w_down):
    w4 = jnp.concatenate(
        [w[:, :, h * HGRN_HEAD_DIM:(h + 1) * HGRN_HEAD_DIM]
         for h in range(HGRN_HEADS) for w in (hgrn_w_q, hgrn_w_f, hgrn_w_i, hgrn_w_g)],
        axis=-1).astype(BF16)
    p = {
        'norm_mix': norm_mix, 'norm_ffn': norm_ffn, 'norm_final': norm_final,
        'conf_w_pw1': conf_w_pw1.astype(BF16), 'conf_b_pw1': conf_b_pw1, 'conf_w_dw': conf_w_dw,
        'conf_b_dw': conf_b_dw, 'conf_ln_g': conf_ln_g, 'conf_ln_b': conf_ln_b,
        'conf_w_pw2': conf_w_pw2.astype(BF16), 'conf_b_pw2': conf_b_pw2,
        'pool_w': pool_w.astype(BF16), 'pool_scale': pool_scale,
        'sconv_w_in': sconv_w_in.astype(BF16), 'sconv_w_dw': sconv_w_dw,
        'sconv_w_out': sconv_w_out.astype(BF16),
        'hgrn_w4': w4,
        'hgrn_w_o': hgrn_w_o.astype(BF16), 'hgrn_lb_logits': hgrn_lb_logits,
        'hgrn_norm_g': hgrn_norm_g,
        'ffn_w_gate': ffn_w_gate.astype(BF16), 'ffn_w_up': ffn_w_up.astype(BF16),
        'ffn_w_dw': ffn_w_dw, 'ffn_b_dw': ffn_b_dw, 'ffn_w_down': ffn_w_down.astype(BF16),
    }
    b = x_prompt.shape[0]
    zeros_like_state = lambda s: jnp.zeros((s.shape[0], b) + s.shape[2:], s.dtype)
    y_p, conf_p, pool_p, sconv_p, hgrn_p, ffn_p = _trunk(
        x_prompt, 0, zeros_like_state(state_conformer_conv), zeros_like_state(state_pool),
        zeros_like_state(state_short_conv), zeros_like_state(state_hgrn),
        zeros_like_state(state_ffn_conv), p)
    y_s, conf_s, pool_s, sconv_s, hgrn_s, ffn_s = _trunk(
        x_sample, PAST_LEN, state_conformer_conv, state_pool, state_short_conv, state_hgrn,
        state_ffn_conv, p)
    return (y_p, y_s, conf_p, conf_s, pool_p, pool_s, sconv_p, sconv_s,
            hgrn_p, hgrn_s, ffn_p, ffn_s)
```

```python
import functools
from typing import NamedTuple

import jax
import jax.numpy as jnp
from jax import lax
from jax.experimental import pallas as pl
from jax.experimental.pallas import tpu as pltpu

D_MODEL = 1024
D_FF = 2816
CONF_WIDTH = 31
POOL_WINDOWS = (2, 4, 8, 16)
POOL_GROUP_DIM = D_MODEL // len(POOL_WINDOWS)
POOL_HIST = max(POOL_WINDOWS) - 1
SCONV_WIDTH = 3
FFN_WIDTH = 3
HGRN_HEAD_DIM = 128
HGRN_HEADS = D_MODEL // HGRN_HEAD_DIM
HGRN_CHUNK = 64
PAST_LEN = 2048
EPS = 1e-6

SUBLANES = 8
LANES = 128
TILE_ROWS = 512
LIGHT_MIXER_TILE_ROWS = 1024
VMEM_LIMIT_BYTES = 56 * 1024 * 1024

F32 = jnp.float32
BF16 = jnp.bfloat16


def _round_up(n, m):
    return (n + m - 1) // m * m


def _rms(x, g):
    return x * lax.rsqrt(jnp.mean(x * x, axis=-1, keepdims=True) + EPS) * g


def _dot(a, b):
    return jnp.dot(a, b, preferred_element_type=F32)


def _dot_nt(a, b):
    return lax.dot_general(a, b, (((1,), (1,)), ((), ())), preferred_element_type=F32)


def _dot_tn(a, b):
    return lax.dot_general(a, b, (((0,), (0,)), ((), ())), preferred_element_type=F32)


def _col_chunks(total, width):
    return [(lo, min(lo + width, total)) for lo in range(0, total, width)]


FFN_PAD = SUBLANES
FFN_CHUNK = 768


def _ffn_kernel(x_ref, hist_ref, ng_ref, wg_ref, wu_ref, wdw_ref, bdw_ref, wd_ref, fin_ref,
                o_ref, nh_ref, cbuf, carry, *, nb, tt, chunks, final_norm):
    m = nb * tt
    h0 = FFN_PAD - (FFN_WIDTH - 1)

    @pl.when(pl.program_id(1) == 0)
    def _():
        carry[:, h0:FFN_PAD, :] = hist_ref[...]

    x = x_ref[...].reshape(m, D_MODEL)
    xn = _rms(x, ng_ref[...]).astype(BF16)
    acc = x

    def up_projections(lo, hi):
        return _dot(xn, wg_ref[:, lo:hi]), _dot(xn, wu_ref[:, lo:hi])

    projected = [up_projections(*c) for c in chunks]
    for ci, (lo, hi) in enumerate(chunks):
        nc = hi - lo
        g, u = projected[ci]
        g = g.reshape(nb, tt, nc)
        cb = cbuf.at[ci % 2]
        cb[:, h0:FFN_PAD, 0:nc] = carry[:, h0:FFN_PAD, lo:hi]
        cb[:, FFN_PAD:FFN_PAD + tt, 0:nc] = g
        g1 = cb[:, FFN_PAD - 1:FFN_PAD - 1 + tt, 0:nc]
        g2 = cb[:, FFN_PAD - 2:FFN_PAD - 2 + tt, 0:nc]
        w = wdw_ref[:, lo:hi]
        c = g * w[2:3] + g1 * w[1:2] + g2 * w[0:1] + bdw_ref[:, lo:hi]
        carry[:, h0:FFN_PAD, lo:hi] = cb[:, h0 + tt:FFN_PAD + tt, 0:nc]
        h = (c * jax.nn.sigmoid(c)).reshape(m, nc) * u
        acc = acc + _dot(h.astype(BF16), wd_ref[lo:hi, :])
    nh_ref[...] = carry[:, h0:FFN_PAD, :]
    if final_norm:
        acc = _rms(acc, fin_ref[...])
    o_ref[...] = acc.reshape(nb, tt, D_MODEL)


CONF_PAD = _round_up(CONF_WIDTH - 1, SUBLANES)
CONF_CHUNK = 256
ROW_STRIDE = 4
ROW_GROUP = ROW_STRIDE * SUBLANES
N_SLABS = D_MODEL // LANES


def _conf_kernel(x_ref, hist_ref, ng_ref, w1_ref, b1_ref, wdw_ref, bdw_ref, lng_ref, lnb_ref,
                 w2_ref, b2_ref, o_ref, nh_ref, ubuf, cbuf, *, nb, tt, chunks):
    m = nb * tt
    hw = CONF_WIDTH - 1
    h0 = CONF_PAD - hw

    @pl.when(pl.program_id(1) == 0)
    def _():
        for j in range(N_SLABS):
            ubuf[j, :, h0:CONF_PAD, :] = hist_ref[:, :, j * LANES:(j + 1) * LANES]

    x = x_ref[...].reshape(m, D_MODEL)
    xn = _rms(x, ng_ref[...]).astype(BF16)
    projected = [(_dot(xn, w1_ref[:, lo:hi]), _dot(xn, w1_ref[:, D_MODEL + lo:D_MODEL + hi]))
                 for lo, hi in chunks]
    for (lo, hi), (a, gt) in zip(chunks, projected):
        a = a + b1_ref[:, lo:hi]
        gt = gt + b1_ref[:, D_MODEL + lo:D_MODEL + hi]
        u = a * jax.nn.sigmoid(gt)
        for l0 in range(lo, hi, LANES):
            ubuf[l0 // LANES, :, CONF_PAD:CONF_PAD + tt, :] = (
                u[:, l0 - lo:l0 - lo + LANES].reshape(nb, tt, LANES))

    def slab(j, carry):
        l0 = pl.multiple_of(j * LANES, LANES)
        w = wdw_ref[:, pl.ds(l0, LANES)]
        wk = [jnp.broadcast_to(w[k:k + 1], (SUBLANES, LANES)) for k in range(CONF_WIDTH)]
        bias = jnp.broadcast_to(bdw_ref[:, pl.ds(l0, LANES)], (SUBLANES, LANES))
        for b in range(nb):
            for r0 in range(0, tt, ROW_GROUP):
                for ph in range(ROW_STRIDE):
                    acc = bias
                    for k in range(CONF_WIDTH):
                        rows = pl.ds(h0 + k + r0 + ph, SUBLANES, stride=ROW_STRIDE)
                        acc = acc + ubuf[j, b, rows, :] * wk[k]
                    cbuf[j, b, pl.ds(r0 + ph, SUBLANES, stride=ROW_STRIDE), :] = acc
        return carry

    lax.fori_loop(0, N_SLABS, slab, 0)

    for j in range(N_SLABS):
        nh_ref[:, :, j * LANES:(j + 1) * LANES] = ubuf[j, :, h0 + tt:CONF_PAD + tt, :]
        ubuf[j, :, h0:CONF_PAD, :] = ubuf[j, :, h0 + tt:CONF_PAD + tt, :]

    c = jnp.concatenate([cbuf[j].reshape(m, LANES) for j in range(N_SLABS)], axis=-1)
    mu = jnp.mean(c, axis=-1, keepdims=True)
    cc = c - mu
    var = jnp.mean(cc * cc, axis=-1, keepdims=True)
    cn = cc * lax.rsqrt(var + EPS) * lng_ref[...] + lnb_ref[...]
    s = cn * jax.nn.sigmoid(cn)
    y = _dot(s.astype(BF16), w2_ref[...]) + b2_ref[...]
    o_ref[...] = (x + y).reshape(nb, tt, D_MODEL)


POOL_PAD = _round_up(POOL_HIST, SUBLANES)


def _pool_kernel(x_ref, hist_ref, ng_ref, wp_ref, sc_ref, o_ref, nh_ref, pbuf, dbuf,
                 *, nb, tt, pos0):
    m = nb * tt
    h0 = POOL_PAD - POOL_HIST
    t = pl.program_id(1)

    @pl.when(t == 0)
    def _():
        for j in range(N_SLABS):
            pbuf[j, :, h0:POOL_PAD, :] = hist_ref[:, :, j * LANES:(j + 1) * LANES]

    x = x_ref[...].reshape(m, D_MODEL)
    xn = _rms(x, ng_ref[...])
    for j in range(N_SLABS):
        pbuf[j, :, POOL_PAD:POOL_PAD + tt, :] = xn[:, j * LANES:(j + 1) * LANES].reshape(nb, tt, LANES)

    tile_rows = ROW_STRIDE * lax.broadcasted_iota(jnp.int32, (SUBLANES, 1), 0)
    slabs_per_group = POOL_GROUP_DIM // LANES
    for gi, win in enumerate(POOL_WINDOWS):
        for r0 in range(0, tt, ROW_GROUP):
            for ph in range(ROW_STRIDE):
                pos = pos0 + t * tt + r0 + ph + tile_rows
                inv = 1.0 / jnp.minimum(pos + 1, win).astype(F32)
                for j in range(gi * slabs_per_group, (gi + 1) * slabs_per_group):
                    for b in range(nb):
                        cur = pbuf[j, b, pl.ds(POOL_PAD + r0 + ph, SUBLANES, stride=ROW_STRIDE), :]
                        acc = cur
                        for i in range(1, win):
                            rows = pl.ds(POOL_PAD + r0 + ph - i, SUBLANES, stride=ROW_STRIDE)
                            acc = acc + pbuf[j, b, rows, :]
                        dbuf[j, b, pl.ds(r0 + ph, SUBLANES, stride=ROW_STRIDE), :] = acc * inv - cur

    for j in range(N_SLABS):
        nh_ref[:, :, j * LANES:(j + 1) * LANES] = pbuf[j, :, h0 + tt:POOL_PAD + tt, :]
        pbuf[j, :, h0:POOL_PAD, :] = pbuf[j, :, h0 + tt:POOL_PAD + tt, :]

    for gi in range(len(POOL_WINDOWS)):
        lo, hi = gi * POOL_GROUP_DIM, (gi + 1) * POOL_GROUP_DIM
        d = jnp.concatenate([dbuf[j].reshape(m, LANES) for j in range(lo // LANES, hi // LANES)],
                            axis=-1).astype(BF16)
        y = _dot(d, wp_ref[gi]) * sc_ref[:, lo:hi]
        o_ref[:, :, lo:hi] = (x[:, lo:hi] + y).reshape(nb, tt, POOL_GROUP_DIM)


SCONV_PAD = SUBLANES
SCONV_CHUNK = 256


def _sconv_kernel(x_ref, hist_ref, ng_ref, win_ref, wdw_ref, wout_ref, o_ref, nh_ref, pbuf,
                  *, nb, tt, chunks):
    m = nb * tt
    h0 = SCONV_PAD - (SCONV_WIDTH - 1)

    @pl.when(pl.program_id(1) == 0)
    def _():
        pbuf[:, h0:SCONV_PAD, :] = hist_ref[...]

    x = x_ref[...].reshape(m, D_MODEL)
    xn = _rms(x, ng_ref[...]).astype(BF16)
    acc = x
    projected = [tuple(_dot(xn, win_ref[:, part * D_MODEL + lo:part * D_MODEL + hi])
                       for part in range(3)) for lo, hi in chunks]
    for (lo, hi), (bg, cg, v) in zip(chunks, projected):
        nc = hi - lo
        p = (cg * v).reshape(nb, tt, nc)
        pbuf[:, SCONV_PAD:SCONV_PAD + tt, lo:hi] = p
        p1 = pbuf[:, SCONV_PAD - 1:SCONV_PAD - 1 + tt, lo:hi]
        p2 = pbuf[:, SCONV_PAD - 2:SCONV_PAD - 2 + tt, lo:hi]
        w = wdw_ref[:, lo:hi]
        c = p * w[2:3] + p1 * w[1:2] + p2 * w[0:1]
        pbuf[:, h0:SCONV_PAD, lo:hi] = pbuf[:, h0 + tt:SCONV_PAD + tt, lo:hi]
        acc = acc + _dot((bg * c.reshape(m, nc)).astype(BF16), wout_ref[lo:hi, :])
    nh_ref[...] = pbuf[:, h0:SCONV_PAD, :]
    o_ref[...] = acc.reshape(nb, tt, D_MODEL)


HGRN_PROJ = 4 * HGRN_HEAD_DIM
HGRN_BLOCK = HGRN_CHUNK
HGRN_HEAD_GROUP = 8


def _hgrn_kernel(x_ref, s0_ref, ng_ref, w4_ref, wo_ref, lbl_ref, hng_ref, o_ref, sout_ref,
                 st_ref, onbuf, *, nb, tt, layer, group):
    m = nb * tt
    hd = HGRN_HEAD_DIM
    ck = HGRN_CHUNK
    t = pl.program_id(1)

    @pl.when(t == 0)
    def _():
        for b in range(nb):
            for h in range(HGRN_HEADS):
                st_ref[b, h] = s0_ref[b, h].T

    x = x_ref[...].reshape(m, D_MODEL)
    xn = _rms(x, ng_ref[...]).astype(BF16)

    lg = lbl_ref[...]
    e = jnp.exp(lg - jnp.max(lg, axis=0, keepdims=True))
    p = e / jnp.sum(e, axis=0, keepdims=True)
    lb_all = p[1:2]
    for i in range(2, layer + 1):
        lb_all = lb_all + p[i:i + 1]
    if layer == 0:
        lb_all = jnp.zeros_like(p[0:1])

    blk = HGRN_BLOCK
    row = lax.broadcasted_iota(jnp.int32, (blk, blk), 0)
    col = lax.broadcasted_iota(jnp.int32, (blk, blk), 1)
    ck_shift = ck.bit_length() - 1
    causal = (row >= col) & ((row >> ck_shift) == (col >> ck_shift))
    tri = causal.astype(BF16)
    hng = hng_ref[...]

    blocks = list(range(0, m, blk))
    chunk_starts = list(range(0, blk, ck))
    for g0 in range(0, HGRN_HEADS, group):
        heads = list(range(g0, g0 + group))
        q, k, v, gate, lf3 = {}, {}, {}, {}, {}
        for h in heads:
            z = _dot(xn, w4_ref[:, h * HGRN_PROJ:(h + 1) * HGRN_PROJ])
            lb = lb_all[:, h * hd:(h + 1) * hd]
            zq = z[:, 0:hd]
            q[h] = zq * jax.nn.sigmoid(zq)
            f = lb + (1.0 - lb) * jax.nn.sigmoid(z[:, hd:2 * hd])
            k[h] = 1.0 - f
            lf = jnp.log(f)
            v[h] = z[:, 2 * hd:3 * hd].astype(BF16)
            zg = z[:, 3 * hd:4 * hd]
            gate[h] = zg * jax.nn.sigmoid(zg)
            hi = lf.astype(BF16)
            r1 = lf - hi.astype(F32)
            mid = r1.astype(BF16)
            lo = (r1 - mid.astype(F32)).astype(BF16)
            lf3[h] = jnp.concatenate([hi, mid, lo], axis=1)
        units = [(h, r0) for h in heads for r0 in blocks]
        bc3 = {u: _dot(tri, lf3[u[0]][u[1]:u[1] + blk]) for u in units}
        qd, kd, kk, lasts = {}, {}, {}, {}
        for h, r0 in units:
            u = (h, r0)
            b3 = bc3[u]
            bc = b3[:, 0:hd] + b3[:, hd:2 * hd] + b3[:, 2 * hd:3 * hd]
            lasts[u] = [bc[c0 + ck - 1:c0 + ck] for c0 in chunk_starts]
            bl = jnp.concatenate([jnp.broadcast_to(l, (ck, hd)) for l in lasts[u]], axis=0)
            q_b = q[h][r0:r0 + blk]
            k_b = k[h][r0:r0 + blk]
            qd[u] = (q_b * jnp.exp(bc)).astype(BF16)
            kd[u] = (k_b * jnp.exp(-bc)).astype(BF16)
            kk[u] = (k_b * jnp.exp(bl - bc)).astype(BF16)
        sc = {u: jnp.where(causal, _dot_nt(qd[u], kd[u]), 0.0).astype(BF16) for u in units}
        o_intra = {u: _dot(sc[u], v[u[0]][u[1]:u[1] + blk]) for u in units}
        upd = {(h, r0, c0): _dot_tn(v[h][r0 + c0:r0 + c0 + ck], kk[(h, r0)][c0:c0 + ck])
               for h, r0 in units for c0 in chunk_starts}
        st_in = {}
        for h in heads:
            cur = {}
            for r0 in blocks:
                for ci, c0 in enumerate(chunk_starts):
                    b = (r0 + c0) // tt
                    if b not in cur:
                        cur[b] = st_ref[b, h]
                    st_in[(h, r0, c0)] = cur[b].astype(BF16)
                    cur[b] = cur[b] * jnp.exp(lasts[(h, r0)][ci]) + upd[(h, r0, c0)]
            for b, s in cur.items():
                st_ref[b, h] = s
        o_inter = {key: _dot_nt(qd[key[:2]][key[2]:key[2] + ck], st_in[key]) for key in st_in}
        for h, r0 in units:
            o = o_intra[(h, r0)] + jnp.concatenate([o_inter[(h, r0, c0)] for c0 in chunk_starts], axis=0)
            on = o * lax.rsqrt(jnp.mean(o * o, axis=-1, keepdims=True) + EPS) * hng
            onbuf[r0:r0 + blk, h * hd:(h + 1) * hd] = (on * gate[h][r0:r0 + blk]).astype(BF16)

    y = _dot(onbuf[...], wo_ref[...])
    o_ref[...] = (x + y).reshape(nb, tt, D_MODEL)

    @pl.when(t == pl.num_programs(1) - 1)
    def _():
        for b in range(nb):
            for h in range(HGRN_HEADS):
                sout_ref[b, h] = st_ref[b, h].T


def _tiling(batch, seq, rows):
    tt = min(seq, rows)
    nb = max(1, min(batch, rows // tt))
    assert seq % tt == 0 and batch % nb == 0
    assert tt % HGRN_CHUNK == 0 and tt % ROW_GROUP == 0 and (nb * tt) % HGRN_BLOCK == 0
    return nb, tt


class _Layer(NamedTuple):
    stacked: jax.Array
    index: int


def _operand(c):
    return c.stacked if isinstance(c, _Layer) else c


def _resident(c):
    if isinstance(c, _Layer):
        shape = c.stacked.shape[1:]
        index = (c.index,) + (0,) * len(shape)
        return pl.BlockSpec((None,) + shape, lambda b, t: index, pipeline_mode=pl.Buffered(1))
    zeros = (0,) * c.ndim
    return pl.BlockSpec(c.shape, lambda b, t: zeros, pipeline_mode=pl.Buffered(1))


def _x_spec(nb, tt):
    return pl.BlockSpec((nb, tt, D_MODEL), lambda b, t: (b, t, 0))


def _state_spec(nb, shape):
    zeros = (0,) * len(shape)
    return pl.BlockSpec((nb,) + shape, lambda b, t: (b,) + zeros)


def _call(body, name, x, state, consts, scratch, extra_out_shape, rows=TILE_ROWS):
    batch, seq, _ = x.shape
    nb, tt = _tiling(batch, seq, rows)
    state_shape = state.shape[1:]
    return pl.pallas_call(
        functools.partial(body, nb=nb, tt=tt),
        name=name,
        grid=(batch // nb, seq // tt),
        in_specs=[_x_spec(nb, tt), _state_spec(nb, state_shape)] + [_resident(c) for c in consts],
        out_specs=[_x_spec(nb, tt), _state_spec(nb, extra_out_shape)],
        out_shape=[jax.ShapeDtypeStruct(x.shape, F32),
                   jax.ShapeDtypeStruct((batch,) + extra_out_shape, F32)],
        scratch_shapes=scratch(nb, tt),
        compiler_params=pltpu.CompilerParams(
            dimension_semantics=("arbitrary", "arbitrary"),
            vmem_limit_bytes=VMEM_LIMIT_BYTES),
    )(x, state, *[_operand(c) for c in consts])


def _row(v):
    return v.reshape(1, -1)


def _ffn(x, hist, ng, wg, wu, wdw, bdw, wd, fin, final_norm):
    body = functools.partial(_ffn_kernel, chunks=_col_chunks(D_FF, FFN_CHUNK), final_norm=final_norm)
    scratch = lambda nb, tt: [pltpu.VMEM((2, nb, FFN_PAD + tt, FFN_CHUNK), F32),
                              pltpu.VMEM((nb, FFN_PAD, D_FF), F32)]
    return _call(body, "conv_ffn", x, hist,
                 [_row(ng), wg, wu, wdw, _row(bdw), wd, _row(fin)], scratch, (FFN_WIDTH - 1, D_FF))


def _conf(x, hist, ng, w1, b1, wdw, bdw, lng, lnb, w2, b2):
    body = functools.partial(_conf_kernel, chunks=_col_chunks(D_MODEL, CONF_CHUNK))
    scratch = lambda nb, tt: [pltpu.VMEM((N_SLABS, nb, CONF_PAD + tt, LANES), F32),
                              pltpu.VMEM((N_SLABS, nb, tt, LANES), F32)]
    return _call(body, "conformer_conv", x, hist,
                 [_row(ng), w1, _row(b1), wdw, _row(bdw), _row(lng), _row(lnb), w2, _row(b2)],
                 scratch, (CONF_WIDTH - 1, D_MODEL), rows=LIGHT_MIXER_TILE_ROWS)


def _pool(x, hist, ng, wp, scale, pos0):
    body = functools.partial(_pool_kernel, pos0=pos0)
    scratch = lambda nb, tt: [pltpu.VMEM((N_SLABS, nb, POOL_PAD + tt, LANES), F32),
                              pltpu.VMEM((N_SLABS, nb, tt, LANES), F32)]
    return _call(body, "pool_mixer", x, hist, [_row(ng), wp, _row(scale)], scratch,
                 (POOL_HIST, D_MODEL), rows=LIGHT_MIXER_TILE_ROWS)


def _sconv(x, hist, ng, win, wdw, wout):
    body = functools.partial(_sconv_kernel, chunks=_col_chunks(D_MODEL, SCONV_CHUNK))
    scratch = lambda nb, tt: [pltpu.VMEM((nb, SCONV_PAD + tt, D_MODEL), F32)]
    return _call(body, "short_conv", x, hist, [_row(ng), win, wdw, wout], scratch,
                 (SCONV_WIDTH - 1, D_MODEL), rows=LIGHT_MIXER_TILE_ROWS)


def _hgrn(x, s0, ng, w4, wo, lb_logits, hng, layer):
    body = functools.partial(_hgrn_kernel, layer=layer, group=HGRN_HEAD_GROUP)
    scratch = lambda nb, tt: [pltpu.VMEM((nb, HGRN_HEADS, HGRN_HEAD_DIM, HGRN_HEAD_DIM), F32),
                              pltpu.VMEM((nb * tt, D_MODEL), BF16)]
    return _call(body, "hgrn2", x, s0, [_row(ng), w4, wo, lb_logits, _row(hng)], scratch,
                 (HGRN_HEADS, HGRN_HEAD_DIM, HGRN_HEAD_DIM))


def _trunk(x, pos0, conf_hist, pool_hist, sconv_hist, hgrn_state, ffn_hist, p):
    depth = p['norm_mix'].shape[0]
    new = {'conf': [], 'pool': [], 'sconv': [], 'hgrn': [], 'ffn': []}
    for i in range(depth):
        mixer, j = i % 4, i // 4
        ng = p['norm_mix'][i]
        if mixer == 0:
            x, h = _conf(x, conf_hist[j], ng, p['conf_w_pw1'][j], p['conf_b_pw1'][j],
                         p['conf_w_dw'][j], p['conf_b_dw'][j], p['conf_ln_g'][j], p['conf_ln_b'][j],
                         p['conf_w_pw2'][j], p['conf_b_pw2'][j])
            new['conf'].append(h)
        elif mixer == 1:
            x, h = _pool(x, pool_hist[j], ng, p['pool_w'][j], p['pool_scale'][j], pos0)
            new['pool'].append(h)
        elif mixer == 2:
            x, h = _sconv(x, sconv_hist[j], ng, p['sconv_w_in'][j], p['sconv_w_dw'][j],
                          p['sconv_w_out'][j])
            new['sconv'].append(h)
        else:
            x, h = _hgrn(x, hgrn_state[j], ng, p['hgrn_w4'][j], p['hgrn_w_o'][j],
                         p['hgrn_lb_logits'], p['hgrn_norm_g'][j], i)
            new['hgrn'].append(h)
        x, h = _ffn(x, ffn_hist[i], p['norm_ffn'][i], _Layer(p['ffn_w_gate'], i),
                    _Layer(p['ffn_w_up'], i), p['ffn_w_dw'][i], p['ffn_b_dw'][i],
                    _Layer(p['ffn_w_down'], i), p['norm_final'],
                    final_norm=(i == depth - 1))
        new['ffn'].append(h)
    return (x,) + tuple(jnp.stack(new[k]) for k in ('conf', 'pool', 'sconv', 'hgrn', 'ffn'))


def kernel(x_prompt, x_sample, state_conformer_conv, state_pool, state_short_conv, state_hgrn, state_ffn_conv, norm_mix, norm_ffn, norm_final, conf_w_pw1, conf_b_pw1, conf_w_dw, conf_b_dw, conf_ln_g, conf_ln_b, conf_w_pw2, conf_b_pw2, pool_w, pool_scale, sconv_w_in, sconv_w_dw, sconv_w_out, hgrn_w_q, hgrn_w_f, hgrn_w_i, hgrn_w_g, hgrn_w_o, hgrn_lb_logits, hgrn_norm_g, ffn_w_gate, ffn_w_up, ffn_w_dw, ffn_b_dw, ffn_w_down):
    w4 = jnp.concatenate(
        [w[:, :, h * HGRN_HEAD_DIM:(h + 1) * HGRN_HEAD_DIM]
         for h in range(HGRN_HEADS) for w in (hgrn_w_q, hgrn_w_f, hgrn_w_i, hgrn_w_g)],
        axis=-1).astype(BF16)
    p = {
        'norm_mix': norm_mix, 'norm_ffn': norm_ffn, 'norm_final': norm_final,
        'conf_w_pw1': conf_w_pw1.astype(BF16), 'conf_b_pw1': conf_b_pw1, 'conf_w_dw': conf_w_dw,
        'conf_b_dw': conf_b_dw, 'conf_ln_g': conf_ln_g, 'conf_ln_b': conf_ln_b,
        'conf_w_pw2': conf_w_pw2.astype(BF16), 'conf_b_pw2': conf_b_pw2,
        'pool_w': pool_w.astype(BF16), 'pool_scale': pool_scale,
        'sconv_w_in': sconv_w_in.astype(BF16), 'sconv_w_dw': sconv_w_dw,
        'sconv_w_out': sconv_w_out.astype(BF16),
        'hgrn_w4': w4,
        'hgrn_w_o': hgrn_w_o.astype(BF16), 'hgrn_lb_logits': hgrn_lb_logits,
        'hgrn_norm_g': hgrn_norm_g,
        'ffn_w_gate': ffn_w_gate.astype(BF16), 'ffn_w_up': ffn_w_up.astype(BF16),
        'ffn_w_dw': ffn_w_dw, 'ffn_b_dw': ffn_b_dw, 'ffn_w_down': ffn_w_down.astype(BF16),
    }
    b = x_prompt.shape[0]
    zeros_like_state = lambda s: jnp.zeros((s.shape[0], b) + s.shape[2:], s.dtype)
    y_p, conf_p, pool_p, sconv_p, hgrn_p, ffn_p = _trunk(
        x_prompt, 0, zeros_like_state(state_conformer_conv), zeros_like_state(state_pool),
        zeros_like_state(state_short_conv), zeros_like_state(state_hgrn),
        zeros_like_state(state_ffn_conv), p)
    y_s, conf_s, pool_s, sconv_s, hgrn_s, ffn_s = _trunk(
        x_sample, PAST_LEN, state_conformer_conv, state_pool, state_short_conv, state_hgrn,
        state_ffn_conv, p)
    return (y_p, y_s, conf_p, conf_s, pool_p, pool_s, sconv_p, sconv_s,
            hgrn_p, hgrn_s, ffn_p, ffn_s)
```

```python
import functools
from typing import NamedTuple

import jax
import jax.numpy as jnp
from jax import lax
from jax.experimental import pallas as pl
from jax.experimental.pallas import tpu as pltpu

D_MODEL = 1024
D_FF = 2816
CONF_WIDTH = 31
POOL_WINDOWS = (2, 4, 8, 16)
POOL_GROUP_DIM = D_MODEL // len(POOL_WINDOWS)
POOL_HIST = max(POOL_WINDOWS) - 1
SCONV_WIDTH = 3
FFN_WIDTH = 3
HGRN_HEAD_DIM = 128
HGRN_HEADS = D_MODEL // HGRN_HEAD_DIM
HGRN_CHUNK = 64
PAST_LEN = 2048
EPS = 1e-6

SUBLANES = 8
LANES = 128
TILE_ROWS = 512
LIGHT_MIXER_TILE_ROWS = 1024
VMEM_LIMIT_BYTES = 56 * 1024 * 1024

F32 = jnp.float32
BF16 = jnp.bfloat16


def _round_up(n, m):
    return (n + m - 1) // m * m


def _rms(x, g):
    return x * lax.rsqrt(jnp.mean(x * x, axis=-1, keepdims=True) + EPS) * g


def _dot(a, b):
    return jnp.dot(a, b, preferred_element_type=F32)


def _dot_nt(a, b):
    return lax.dot_general(a, b, (((1,), (1,)), ((), ())), preferred_element_type=F32)


def _dot_tn(a, b):
    return lax.dot_general(a, b, (((0,), (0,)), ((), ())), preferred_element_type=F32)


def _col_chunks(total, width):
    return [(lo, min(lo + width, total)) for lo in range(0, total, width)]


FFN_PAD = SUBLANES
FFN_CHUNK = 768


def _ffn_kernel(x_ref, hist_ref, ng_ref, wg_ref, wu_ref, wdw_ref, bdw_ref, wd_ref, fin_ref,
                o_ref, nh_ref, cbuf, carry, *, nb, tt, chunks, final_norm):
    m = nb * tt
    h0 = FFN_PAD - (FFN_WIDTH - 1)

    @pl.when(pl.program_id(1) == 0)
    def _():
        carry[:, h0:FFN_PAD, :] = hist_ref[...]

    x = x_ref[...].reshape(m, D_MODEL)
    xn = _rms(x, ng_ref[...]).astype(BF16)
    acc = x

    def up_projections(lo, hi):
        return _dot(xn, wg_ref[:, lo:hi]), _dot(xn, wu_ref[:, lo:hi])

    projected = [up_projections(*c) for c in chunks]
    for ci, (lo, hi) in enumerate(chunks):
        nc = hi - lo
        g, u = projected[ci]
        g = g.reshape(nb, tt, nc)
        cb = cbuf.at[ci % 2]
        cb[:, h0:FFN_PAD, 0:nc] = carry[:, h0:FFN_PAD, lo:hi]
        cb[:, FFN_PAD:FFN_PAD + tt, 0:nc] = g
        g1 = cb[:, FFN_PAD - 1:FFN_PAD - 1 + tt, 0:nc]
        g2 = cb[:, FFN_PAD - 2:FFN_PAD - 2 + tt, 0:nc]
        w = wdw_ref[:, lo:hi]
        c = g * w[2:3] + g1 * w[1:2] + g2 * w[0:1] + bdw_ref[:, lo:hi]
        carry[:, h0:FFN_PAD, lo:hi] = cb[:, h0 + tt:FFN_PAD + tt, 0:nc]
        h = (c * jax.nn.sigmoid(c)).reshape(m, nc) * u
        acc = acc + _dot(h.astype(BF16), wd_ref[lo:hi, :])
    nh_ref[...] = carry[:, h0:FFN_PAD, :]
    if final_norm:
        acc = _rms(acc, fin_ref[...])
    o_ref[...] = acc.reshape(nb, tt, D_MODEL)


CONF_PAD = _round_up(CONF_WIDTH - 1, SUBLANES)
CONF_CHUNK = 256
ROW_STRIDE = 4
ROW_GROUP = ROW_STRIDE * SUBLANES
N_SLABS = D_MODEL // LANES


def _conf_kernel(x_ref, hist_ref, ng_ref, w1_ref, b1_ref, wdw_ref, bdw_ref, lng_ref, lnb_ref,
                 w2_ref, b2_ref, o_ref, nh_ref, ubuf, cbuf, *, nb, tt, chunks):
    m = nb * tt
    hw = CONF_WIDTH - 1
    h0 = CONF_PAD - hw

    @pl.when(pl.program_id(1) == 0)
    def _():
        for j in range(N_SLABS):
            ubuf[j, :, h0:CONF_PAD, :] = hist_ref[:, :, j * LANES:(j + 1) * LANES]

    x = x_ref[...].reshape(m, D_MODEL)
    xn = _rms(x, ng_ref[...]).astype(BF16)
    projected = [(_dot(xn, w1_ref[:, lo:hi]), _dot(xn, w1_ref[:, D_MODEL + lo:D_MODEL + hi]))
                 for lo, hi in chunks]
    for (lo, hi), (a, gt) in zip(chunks, projected):
        a = a + b1_ref[:, lo:hi]
        gt = gt + b1_ref[:, D_MODEL + lo:D_MODEL + hi]
        u = a * jax.nn.sigmoid(gt)
        for l0 in range(lo, hi, LANES):
            ubuf[l0 // LANES, :, CONF_PAD:CONF_PAD + tt, :] = (
                u[:, l0 - lo:l0 - lo + LANES].reshape(nb, tt, LANES))

    def slab(j, carry):
        l0 = pl.multiple_of(j * LANES, LANES)
        w = wdw_ref[:, pl.ds(l0, LANES)]
        wk = [jnp.broadcast_to(w[k:k + 1], (SUBLANES, LANES)) for k in range(CONF_WIDTH)]
        bias = jnp.broadcast_to(bdw_ref[:, pl.ds(l0, LANES)], (SUBLANES, LANES))
        for b in range(nb):
            for r0 in range(0, tt, ROW_GROUP):
                for ph in range(ROW_STRIDE):
                    acc = bias
                    for k in range(CONF_WIDTH):
                        rows = pl.ds(h0 + k + r0 + ph, SUBLANES, stride=ROW_STRIDE)
                        acc = acc + ubuf[j, b, rows, :] * wk[k]
                    cbuf[j, b, pl.ds(r0 + ph, SUBLANES, stride=ROW_STRIDE), :] = acc
        return carry

    lax.fori_loop(0, N_SLABS, slab, 0)

    for j in range(N_SLABS):
        nh_ref[:, :, j * LANES:(j + 1) * LANES] = ubuf[j, :, h0 + tt:CONF_PAD + tt, :]
        ubuf[j, :, h0:CONF_PAD, :] = ubuf[j, :, h0 + tt:CONF_PAD + tt, :]

    c = jnp.concatenate([cbuf[j].reshape(m, LANES) for j in range(N_SLABS)], axis=-1)
    mu = jnp.mean(c, axis=-1, keepdims=True)
    cc = c - mu
    var = jnp.mean(cc * cc, axis=-1, keepdims=True)
    cn = cc * lax.rsqrt(var + EPS) * lng_ref[...] + lnb_ref[...]
    s = cn * jax.nn.sigmoid(cn)
    y = _dot(s.astype(BF16), w2_ref[...]) + b2_ref[...]
    o_ref[...] = (x + y).reshape(nb, tt, D_MODEL)


POOL_PAD = _round_up(POOL_HIST, SUBLANES)


def _pool_kernel(x_ref, hist_ref, ng_ref, wp_ref, sc_ref, o_ref, nh_ref, pbuf, dbuf,
                 *, nb, tt, pos0):
    m = nb * tt
    h0 = POOL_PAD - POOL_HIST
    t = pl.program_id(1)

    @pl.when(t == 0)
    def _():
        for j in range(N_SLABS):
            pbuf[j, :, h0:POOL_PAD, :] = hist_ref[:, :, j * LANES:(j + 1) * LANES]

    x = x_ref[...].reshape(m, D_MODEL)
    xn = _rms(x, ng_ref[...])
    for j in range(N_SLABS):
        pbuf[j, :, POOL_PAD:POOL_PAD + tt, :] = xn[:, j * LANES:(j + 1) * LANES].reshape(nb, tt, LANES)

    tile_rows = ROW_STRIDE * lax.broadcasted_iota(jnp.int32, (SUBLANES, 1), 0)
    slabs_per_group = POOL_GROUP_DIM // LANES
    for gi, win in enumerate(POOL_WINDOWS):
        for r0 in range(0, tt, ROW_GROUP):
            for ph in range(ROW_STRIDE):
                pos = pos0 + t * tt + r0 + ph + tile_rows
                inv = 1.0 / jnp.minimum(pos + 1, win).astype(F32)
                for j in range(gi * slabs_per_group, (gi + 1) * slabs_per_group):
                    for b in range(nb):
                        cur = pbuf[j, b, pl.ds(POOL_PAD + r0 + ph, SUBLANES, stride=ROW_STRIDE), :]
                        acc = cur
                        for i in range(1, win):
                            rows = pl.ds(POOL_PAD + r0 + ph - i, SUBLANES, stride=ROW_STRIDE)
                            acc = acc + pbuf[j, b, rows, :]
                        dbuf[j, b, pl.ds(r0 + ph, SUBLANES, stride=ROW_STRIDE), :] = acc * inv - cur

    for j in range(N_SLABS):
        nh_ref[:, :, j * LANES:(j + 1) * LANES] = pbuf[j, :, h0 + tt:POOL_PAD + tt, :]
        pbuf[j, :, h0:POOL_PAD, :] = pbuf[j, :, h0 + tt:POOL_PAD + tt, :]

    for gi in range(len(POOL_WINDOWS)):
        lo, hi = gi * POOL_GROUP_DIM, (gi + 1) * POOL_GROUP_DIM
        d = jnp.concatenate([dbuf[j].reshape(m, LANES) for j in range(lo // LANES, hi // LANES)],
                            axis=-1).astype(BF16)
        y = _dot(d, wp_ref[gi]) * sc_ref[:, lo:hi]
        o_ref[:, :, lo:hi] = (x[:, lo:hi] + y).reshape(nb, tt, POOL_GROUP_DIM)


SCONV_PAD = SUBLANES
SCONV_CHUNK = 256


def _sconv_kernel(x_ref, hist_ref, ng_ref, win_ref, wdw_ref, wout_ref, o_ref, nh_ref, pbuf,
                  *, nb, tt, chunks):
    m = nb * tt
    h0 = SCONV_PAD - (SCONV_WIDTH - 1)

    @pl.when(pl.program_id(1) == 0)
    def _():
        pbuf[:, h0:SCONV_PAD, :] = hist_ref[...]

    x = x_ref[...].reshape(m, D_MODEL)
    xn = _rms(x, ng_ref[...]).astype(BF16)
    acc = x
    projected = [tuple(_dot(xn, win_ref[:, part * D_MODEL + lo:part * D_MODEL + hi])
                       for part in range(3)) for lo, hi in chunks]
    for (lo, hi), (bg, cg, v) in zip(chunks, projected):
        nc = hi - lo
        p = (cg * v).reshape(nb, tt, nc)
        pbuf[:, SCONV_PAD:SCONV_PAD + tt, lo:hi] = p
        p1 = pbuf[:, SCONV_PAD - 1:SCONV_PAD - 1 + tt, lo:hi]
        p2 = pbuf[:, SCONV_PAD - 2:SCONV_PAD - 2 + tt, lo:hi]
        w = wdw_ref[:, lo:hi]
        c = p * w[2:3] + p1 * w[1:2] + p2 * w[0:1]
        pbuf[:, h0:SCONV_PAD, lo:hi] = pbuf[:, h0 + tt:SCONV_PAD + tt, lo:hi]
        acc = acc + _dot((bg * c.reshape(m, nc)).astype(BF16), wout_ref[lo:hi, :])
    nh_ref[...] = pbuf[:, h0:SCONV_PAD, :]
    o_ref[...] = acc.reshape(nb, tt, D_MODEL)


HGRN_PROJ = 4 * HGRN_HEAD_DIM
HGRN_BLOCK = HGRN_CHUNK
HGRN_HEAD_GROUP = 8


def _hgrn_kernel(x_ref, s0_ref, ng_ref, w4_ref, wo_ref, lbl_ref, hng_ref, o_ref, sout_ref,
                 st_ref, onbuf, *, nb, tt, layer, group):
    m = nb * tt
    hd = HGRN_HEAD_DIM
    ck = HGRN_CHUNK
    t = pl.program_id(1)

    @pl.when(t == 0)
    def _():
        for b in range(nb):
            for h in range(HGRN_HEADS):
                st_ref[b, h] = s0_ref[b, h].T

    x = x_ref[...].reshape(m, D_MODEL)
    xn = _rms(x, ng_ref[...]).astype(BF16)

    lg = lbl_ref[...]
    e = jnp.exp(lg - jnp.max(lg, axis=0, keepdims=True))
    p = e / jnp.sum(e, axis=0, keepdims=True)
    lb_all = p[1:2]
    for i in range(2, layer + 1):
        lb_all = lb_all + p[i:i + 1]
    if layer == 0:
        lb_all = jnp.zeros_like(p[0:1])

    blk = HGRN_BLOCK
    row = lax.broadcasted_iota(jnp.int32, (blk, blk), 0)
    col = lax.broadcasted_iota(jnp.int32, (blk, blk), 1)
    ck_shift = ck.bit_length() - 1
    causal = (row >= col) & ((row >> ck_shift) == (col >> ck_shift))
    tri = causal.astype(BF16)
    hng = hng_ref[...]

    blocks = list(range(0, m, blk))
    chunk_starts = list(range(0, blk, ck))
    for g0 in range(0, HGRN_HEADS, group):
        heads = list(range(g0, g0 + group))
        q, k, v, gate, lf3 = {}, {}, {}, {}, {}
        for h in heads:
            z = _dot(xn, w4_ref[:, h * HGRN_PROJ:(h + 1) * HGRN_PROJ])
            lb = lb_all[:, h * hd:(h + 1) * hd]
            zq = z[:, 0:hd]
            q[h] = zq * jax.nn.sigmoid(zq)
            f = lb + (1.0 - lb) * jax.nn.sigmoid(z[:, hd:2 * hd])
            k[h] = 1.0 - f
            lf = jnp.log(f)
            v[h] = z[:, 2 * hd:3 * hd].astype(BF16)
            zg = z[:, 3 * hd:4 * hd]
            gate[h] = zg * jax.nn.sigmoid(zg)
            hi = lf.astype(BF16)
            r1 = lf - hi.astype(F32)
            mid = r1.astype(BF16)
            lo = (r1 - mid.astype(F32)).astype(BF16)
            lf3[h] = jnp.concatenate([hi, mid, lo], axis=1)
        units = [(h, r0) for h in heads for r0 in blocks]
        bc3 = {u: _dot(tri, lf3[u[0]][u[1]:u[1] + blk]) for u in units}
        qd, kd, kk, lasts = {}, {}, {}, {}
        for h, r0 in units:
            u = (h, r0)
            b3 = bc3[u]
            bc = b3[:, 0:hd] + b3[:, hd:2 * hd] + b3[:, 2 * hd:3 * hd]
            lasts[u] = [bc[c0 + ck - 1:c0 + ck] for c0 in chunk_starts]
            bl = jnp.concatenate([jnp.broadcast_to(l, (ck, hd)) for l in lasts[u]], axis=0)
            q_b = q[h][r0:r0 + blk]
            k_b = k[h][r0:r0 + blk]
            qd[u] = (q_b * jnp.exp(bc)).astype(BF16)
            kd[u] = (k_b * jnp.exp(-bc)).astype(BF16)
            kk[u] = (k_b * jnp.exp(bl - bc)).astype(BF16)
        sc = {u: jnp.where(causal, _dot_nt(qd[u], kd[u]), 0.0).astype(BF16) for u in units}
        o_intra = {u: _dot(sc[u], v[u[0]][u[1]:u[1] + blk]) for u in units}
        upd = {(h, r0, c0): _dot_tn(v[h][r0 + c0:r0 + c0 + ck], kk[(h, r0)][c0:c0 + ck])
               for h, r0 in units for c0 in chunk_starts}
        st_in = {}
        for h in heads:
            cur = {}
            for r0 in blocks:
                for ci, c0 in enumerate(chunk_starts):
                    b = (r0 + c0) // tt
                    if b not in cur:
                        cur[b] = st_ref[b, h]
                    st_in[(h, r0, c0)] = cur[b].astype(BF16)
                    cur[b] = cur[b] * jnp.exp(lasts[(h, r0)][ci]) + upd[(h, r0, c0)]
            for b, s in cur.items():
                st_ref[b, h] = s
        o_inter = {key: _dot_nt(qd[key[:2]][key[2]:key[2] + ck], st_in[key]) for key in st_in}
        for h, r0 in units:
            o = o_intra[(h, r0)] + jnp.concatenate([o_inter[(h, r0, c0)] for c0 in chunk_starts], axis=0)
            on = o * lax.rsqrt(jnp.mean(o * o, axis=-1, keepdims=True) + EPS) * hng
            onbuf[r0:r0 + blk, h * hd:(h + 1) * hd] = (on * gate[h][r0:r0 + blk]).astype(BF16)

    y = _dot(onbuf[...], wo_ref[...])
    o_ref[...] = (x + y).reshape(nb, tt, D_MODEL)

    @pl.when(t == pl.num_programs(1) - 1)
    def _():
        for b in range(nb):
            for h in range(HGRN_HEADS):
                sout_ref[b, h] = st_ref[b, h].T


def _tiling(batch, seq, rows):
    tt = min(seq, rows)
    nb = max(1, min(batch, rows // tt))
    assert seq % tt == 0 and batch % nb == 0
    assert tt % HGRN_CHUNK == 0 and tt % ROW_GROUP == 0 and (nb * tt) % HGRN_BLOCK == 0
    return nb, tt


class _Layer(NamedTuple):
    stacked: jax.Array
    index: int


def _operand(c):
    return c.stacked if isinstance(c, _Layer) else c


def _resident(c):
    if isinstance(c, _Layer):
        shape = c.stacked.shape[1:]
        index = (c.index,) + (0,) * len(shape)
        return pl.BlockSpec((None,) + shape, lambda b, t: index, pipeline_mode=pl.Buffered(1))
    zeros = (0,) * c.ndim
    return pl.BlockSpec(c.shape, lambda b, t: zeros, pipeline_mode=pl.Buffered(1))


def _x_spec(nb, tt):
    return pl.BlockSpec((nb, tt, D_MODEL), lambda b, t: (b, t, 0))


def _state_spec(nb, shape):
    zeros = (0,) * len(shape)
    return pl.BlockSpec((nb,) + shape, lambda b, t: (b,) + zeros)


def _call(body, name, x, state, consts, scratch, extra_out_shape, rows=TILE_ROWS):
    batch, seq, _ = x.shape
    nb, tt = _tiling(batch, seq, rows)
    state_shape = state.shape[1:]
    return pl.pallas_call(
        functools.partial(body, nb=nb, tt=tt),
        name=name,
        grid=(batch // nb, seq // tt),
        in_specs=[_x_spec(nb, tt), _state_spec(nb, state_shape)] + [_resident(c) for c in consts],
        out_specs=[_x_spec(nb, tt), _state_spec(nb, extra_out_shape)],
        out_shape=[jax.ShapeDtypeStruct(x.shape, F32),
                   jax.ShapeDtypeStruct((batch,) + extra_out_shape, F32)],
        scratch_shapes=scratch(nb, tt),
        compiler_params=pltpu.CompilerParams(
            dimension_semantics=("arbitrary", "arbitrary"),
            vmem_limit_bytes=VMEM_LIMIT_BYTES),
    )(x, state, *[_operand(c) for c in consts])


def _row(v):
    return v.reshape(1, -1)


def _ffn(x, hist, ng, wg, wu, wdw, bdw, wd, fin, final_norm):
    body = functools.partial(_ffn_kernel, chunks=_col_chunks(D_FF, FFN_CHUNK), final_norm=final_norm)
    scratch = lambda nb, tt: [pltpu.VMEM((2, nb, FFN_PAD + tt, FFN_CHUNK), F32),
                              pltpu.VMEM((nb, FFN_PAD, D_FF), F32)]
    return _call(body, "conv_ffn", x, hist,
                 [_row(ng), wg, wu, wdw, _row(bdw), wd, _row(fin)], scratch, (FFN_WIDTH - 1, D_FF))


def _conf(x, hist, ng, w1, b1, wdw, bdw, lng, lnb, w2, b2):
    body = functools.partial(_conf_kernel, chunks=_col_chunks(D_MODEL, CONF_CHUNK))
    scratch = lambda nb, tt: [pltpu.VMEM((N_SLABS, nb, CONF_PAD + tt, LANES), F32),
                              pltpu.VMEM((N_SLABS, nb, tt, LANES), F32)]
    return _call(body, "conformer_conv", x, hist,
                 [_row(ng), w1, _row(b1), wdw, _row(bdw), _row(lng), _row(lnb), w2, _row(b2)],
                 scratch, (CONF_WIDTH - 1, D_MODEL), rows=LIGHT_MIXER_TILE_ROWS)


def _pool(x, hist, ng, wp, scale, pos0):
    body = functools.partial(_pool_kernel, pos0=pos0)
    scratch = lambda nb, tt: [pltpu.VMEM((N_SLABS, nb, POOL_PAD + tt, LANES), F32),
                              pltpu.VMEM((N_SLABS, nb, tt, LANES), F32)]
    return _call(body, "pool_mixer", x, hist, [_row(ng), wp, _row(scale)], scratch,
                 (POOL_HIST, D_MODEL), rows=LIGHT_MIXER_TILE_ROWS)


def _pool_ffn(x, pool_hist, ffn_hist, pool_ng, wp, scale, pos0, ffn_ng, wg, wu, wdw, bdw, wd, fin,
              final_norm):
    batch, seq, _ = x.shape
    nb, tt = _tiling(batch, seq, TILE_ROWS)
    pool_consts = [_row(pool_ng), wp, _row(scale)]
    ffn_consts = [_row(ffn_ng), wg, wu, wdw, _row(bdw), wd, _row(fin)]
    chunks = _col_chunks(D_FF, FFN_CHUNK)
    n_pool, n_ffn = len(pool_consts), len(ffn_consts)

    def body(x_ref, ph_ref, fh_ref, *rest):
        pc, fc = rest[:n_pool], rest[n_pool:n_pool + n_ffn]
        o_ref, pnh_ref, fnh_ref, pbuf, dbuf, mid, cbuf, carry = rest[n_pool + n_ffn:]
        _pool_kernel(x_ref, ph_ref, *pc, mid, pnh_ref, pbuf, dbuf, nb=nb, tt=tt, pos0=pos0)
        _ffn_kernel(mid, fh_ref, *fc, o_ref, fnh_ref, cbuf, carry, nb=nb, tt=tt, chunks=chunks,
                    final_norm=final_norm)

    consts = pool_consts + ffn_consts
    return pl.pallas_call(
        body,
        name="pool_conv_ffn",
        grid=(batch // nb, seq // tt),
        in_specs=[_x_spec(nb, tt), _state_spec(nb, pool_hist.shape[1:]),
                  _state_spec(nb, ffn_hist.shape[1:])] + [_resident(c) for c in consts],
        out_specs=[_x_spec(nb, tt), _state_spec(nb, (POOL_HIST, D_MODEL)),
                   _state_spec(nb, (FFN_WIDTH - 1, D_FF))],
        out_shape=[jax.ShapeDtypeStruct(x.shape, F32),
                   jax.ShapeDtypeStruct((batch, POOL_HIST, D_MODEL), F32),
                   jax.ShapeDtypeStruct((batch, FFN_WIDTH - 1, D_FF), F32)],
        scratch_shapes=[pltpu.VMEM((N_SLABS, nb, POOL_PAD + tt, LANES), F32),
                        pltpu.VMEM((N_SLABS, nb, tt, LANES), F32),
                        pltpu.VMEM((nb, tt, D_MODEL), F32),
                        pltpu.VMEM((2, nb, FFN_PAD + tt, FFN_CHUNK), F32),
                        pltpu.VMEM((nb, FFN_PAD, D_FF), F32)],
        compiler_params=pltpu.CompilerParams(
            dimension_semantics=("arbitrary", "arbitrary"),
            vmem_limit_bytes=VMEM_LIMIT_BYTES),
    )(x, pool_hist, ffn_hist, *[_operand(c) for c in consts])


def _sconv(x, hist, ng, win, wdw, wout):
    body = functools.partial(_sconv_kernel, chunks=_col_chunks(D_MODEL, SCONV_CHUNK))
    scratch = lambda nb, tt: [pltpu.VMEM((nb, SCONV_PAD + tt, D_MODEL), F32)]
    return _call(body, "short_conv", x, hist, [_row(ng), win, wdw, wout], scratch,
                 (SCONV_WIDTH - 1, D_MODEL), rows=LIGHT_MIXER_TILE_ROWS)


def _hgrn(x, s0, ng, w4, wo, lb_logits, hng, layer):
    body = functools.partial(_hgrn_kernel, layer=layer, group=HGRN_HEAD_GROUP)
    scratch = lambda nb, tt: [pltpu.VMEM((nb, HGRN_HEADS, HGRN_HEAD_DIM, HGRN_HEAD_DIM), F32),
                              pltpu.VMEM((nb * tt, D_MODEL), BF16)]
    return _call(body, "hgrn2", x, s0, [_row(ng), w4, wo, lb_logits, _row(hng)], scratch,
                 (HGRN_HEADS, HGRN_HEAD_DIM, HGRN_HEAD_DIM))


def _trunk(x, pos0, conf_hist, pool_hist, sconv_hist, hgrn_state, ffn_hist, p):
    depth = p['norm_mix'].shape[0]
    new = {'conf': [], 'pool': [], 'sconv': [], 'hgrn': [], 'ffn': []}
    for i in range(depth):
        mixer, j = i % 4, i // 4
        ng = p['norm_mix'][i]
        if mixer == 0:
            x, h = _conf(x, conf_hist[j], ng, p['conf_w_pw1'][j], p['conf_b_pw1'][j],
                         p['conf_w_dw'][j], p['conf_b_dw'][j], p['conf_ln_g'][j], p['conf_ln_b'][j],
                         p['conf_w_pw2'][j], p['conf_b_pw2'][j])
            new['conf'].append(h)
        elif mixer == 1:
            x, h, hf = _pool_ffn(x, pool_hist[j], ffn_hist[i], ng, p['pool_w'][j], p['pool_scale'][j],
                                 pos0, p['norm_ffn'][i], _Layer(p['ffn_w_gate'], i),
                                 _Layer(p['ffn_w_up'], i), p['ffn_w_dw'][i], p['ffn_b_dw'][i],
                                 _Layer(p['ffn_w_down'], i), p['norm_final'],
                                 final_norm=(i == depth - 1))
            new['pool'].append(h)
            new['ffn'].append(hf)
            continue
        elif mixer == 2:
            x, h = _sconv(x, sconv_hist[j], ng, p['sconv_w_in'][j], p['sconv_w_dw'][j],
                          p['sconv_w_out'][j])
            new['sconv'].append(h)
        else:
            x, h = _hgrn(x, hgrn_state[j], ng, p['hgrn_w4'][j], p['hgrn_w_o'][j],
                         p['hgrn_lb_logits'], p['hgrn_norm_g'][j], i)
            new['hgrn'].append(h)
        x, h = _ffn(x, ffn_hist[i], p['norm_ffn'][i], _Layer(p['ffn_w_gate'], i),
                    _Layer(p['ffn_w_up'], i), p['ffn_w_dw'][i], p['ffn_b_dw'][i],
                    _Layer(p['ffn_w_down'], i), p['norm_final'],
                    final_norm=(i == depth - 1))
        new['ffn'].append(h)
    return (x,) + tuple(jnp.stack(new[k]) for k in ('conf', 'pool', 'sconv', 'hgrn', 'ffn'))


def kernel(x_prompt, x_sample, state_conformer_conv, state_pool, state_short_conv, state_hgrn, state_ffn_conv, norm_mix, norm_ffn, norm_final, conf_w_pw1, conf_b_pw1, conf_w_dw, conf_b_dw, conf_ln_g, conf_ln_b, conf_w_pw2, conf_b_pw2, pool_w, pool_scale, sconv_w_in, sconv_w_dw, sconv_w_out, hgrn_w_q, hgrn_w_f, hgrn_w_i, hgrn_w_g, hgrn_w_o, hgrn_lb_logits, hgrn_norm_g, ffn_w_gate, ffn_w_up, ffn_w_dw, ffn_b_dw, ffn_w_down):
    w4 = jnp.concatenate(
        [w[:, :, h * HGRN_HEAD_DIM:(h + 1) * HGRN_HEAD_DIM]
         for h in range(HGRN_HEADS) for w in (hgrn_w_q, hgrn_w_f, hgrn_w_i, hgrn_w_g)],
        axis=-1).astype(BF16)
    p = {
        'norm_mix': norm_mix, 'norm_ffn': norm_ffn, 'norm_final': norm_final,
        'conf_w_pw1': conf_w_pw1.astype(BF16), 'conf_b_pw1': conf_b_pw1, 'conf_w_dw': conf_w_dw,
        'conf_b_dw': conf_b_dw, 'conf_ln_g': conf_ln_g, 'conf_ln_b': conf_ln_b,
        'conf_w_pw2': conf_w_pw2.astype(BF16), 'conf_b_pw2': conf_b_pw2,
        'pool_w': pool_w.astype(BF16), 'pool_scale': pool_scale,
        'sconv_w_in': sconv_w_in.astype(BF16), 'sconv_w_dw': sconv_w_dw,
        'sconv_w_out': sconv_w_out.astype(BF16),
        'hgrn_w4': w4,
        'hgrn_w_o': hgrn_w_o.astype(BF16), 'hgrn_lb_logits': hgrn_lb_logits,
        'hgrn_norm_g': hgrn_norm_g,
        'ffn_w_gate': ffn_w_gate.astype(BF16), 'ffn_w_up': ffn_w_up.astype(BF16),
        'ffn_w_dw': ffn_w_dw, 'ffn_b_dw': ffn_b_dw, 'ffn_w_down': ffn_w_down.astype(BF16),
    }
    b = x_prompt.shape[0]
    zeros_like_state = lambda s: jnp.zeros((s.shape[0], b) + s.shape[2:], s.dtype)
    y_p, conf_p, pool_p, sconv_p, hgrn_p, ffn_p = _trunk(
        x_prompt, 0, zeros_like_state(state_conformer_conv), zeros_like_state(state_pool),
        zeros_like_state(state_short_conv), zeros_like_state(state_hgrn),
        zeros_like_state(state_ffn_conv), p)
    y_s, conf_s, pool_s, sconv_s, hgrn_s, ffn_s = _trunk(
        x_sample, PAST_LEN, state_conformer_conv, state_pool, state_short_conv, state_hgrn,
        state_ffn_conv, p)
    return (y_p, y_s, conf_p, conf_s, pool_p, pool_s, sconv_p, sconv_s,
            hgrn_p, hgrn_s, ffn_p, ffn_s)
```
